```python
import math
import jax
import jax.numpy as jnp
from jax import lax
import numpy as np

D_MODEL = 1024
BATCH = 16
SEQ = 4096
DEPTH = 1
DEC_BATCH = 128
DEC_SEQ = 1
PAST_LEN = 8192
PAGE_SIZE = 128

MIX_WIDTH = D_MODEL
ATTN_WIDTH = MIX_WIDTH // 2
ATTN_HEAD_DIM = 64
ATTN_HEADS = ATTN_WIDTH // ATTN_HEAD_DIM
MLSTM_WIDTH = MIX_WIDTH - ATTN_WIDTH
MLSTM_HEADS = 4
MLSTM_HEAD_DIM = MLSTM_WIDTH // MLSTM_HEADS
DILATED_PAIRS = ((128, 1), (512, 4), (2048, 16))
WINDOW_MAX = 2048
ROT_DIM = ATTN_HEAD_DIM // 4
ROPE_THETA = 500000.0
MLSTM_CHUNK = 64
D_FF = 2816
FORGET_BIAS = 3.0
EPS = 1e-6
IN_COLS = 3 * ATTN_WIDTH + 4 * MLSTM_WIDTH + 2 * MLSTM_HEADS

kernel_name = 'hymba_mlstm_dilated_swa_macaron_step'


def rmsnorm(x, g):
    xf = x.astype(jnp.float32)
    y = xf * lax.rsqrt(jnp.mean(xf * xf, axis=-1, keepdims=True) + EPS) * g.astype(jnp.float32)
    return y.astype(x.dtype)


def head_rmsnorm(x, g):
    y = x * lax.rsqrt(jnp.mean(x * x, axis=-1, keepdims=True) + EPS)
    return y.reshape(x.shape[0], x.shape[1], -1) * g.astype(jnp.float32)


def macaron_ffn(x, g, w_gate, w_up, w_down):
    h = rmsnorm(x, g)
    return x + 0.5 * ((jax.nn.silu(h @ w_gate) * (h @ w_up)) @ w_down)


def partial_rope(x, pos):
    half = ROT_DIM // 2
    inv = jnp.power(jnp.float32(ROPE_THETA), -jnp.arange(half, dtype=jnp.float32) * 2.0 / ROT_DIM)
    ang = pos.astype(jnp.float32)[:, None] * inv[None, :]
    cos = jnp.cos(ang)[None, :, None, :]
    sin = jnp.sin(ang)[None, :, None, :]
    x1 = x[..., :half]
    x2 = x[..., half:ROT_DIM]
    return jnp.concatenate([x1 * cos - x2 * sin, x2 * cos + x1 * sin, x[..., ROT_DIM:]], axis=-1)


def dilated_branch_prompt(q, k, v, dil, steps):
    B, S, H, hd = q.shape
    blk = steps
    span = dil * blk
    s_pad = -(-S // span) * span
    nb = s_pad // span

    def to_classes(x):
        x = jnp.pad(x, ((0, 0), (0, s_pad - S), (0, 0), (0, 0)))
        x = jnp.moveaxis(x.reshape(B, s_pad // dil, dil, H, hd), 2, 1)
        return x.reshape(B * dil, nb, blk, H, hd)

    def from_classes(x):
        rest = x.shape[3:]
        x = jnp.moveaxis(x.reshape(B, dil, s_pad // dil, *rest), 1, 2)
        return x.reshape(B, s_pad, *rest)[:, :S]

    def band(x):
        prev = jnp.pad(x, ((0, 0), (1, 0), (0, 0), (0, 0), (0, 0)))[:, :-1]
        return jnp.concatenate([prev, x], axis=2)

    qc = to_classes(q)
    kb = band(to_classes(k))
    vb = band(to_classes(v))
    s = jnp.einsum('znqhd,znkhd->znhqk', qc, kb)
    dist = jnp.arange(blk)[:, None] + blk - jnp.arange(2 * blk)[None, :]
    in_band = (dist >= 0) & (dist <= steps)
    has_prev = (jnp.arange(nb)[:, None, None] > 0) | (jnp.arange(2 * blk)[None, None, :] >= blk)
    mask = (in_band[None] & has_prev)[None, :, None]
    s = jnp.where(mask, s, -jnp.inf)
    m = jnp.max(s, axis=-1)
    p = jnp.exp(s - m[..., None])
    l = jnp.sum(p, axis=-1)
    o = jnp.einsum('znhqk,znkhd->znqhd', p, vb)
    return from_classes(o), from_classes(jnp.swapaxes(m, 2, 3)), from_classes(jnp.swapaxes(l, 2, 3))


def dilated_branch_sample(q, k_all, v_all, n_buf, dil, steps):
    S = q.shape[1]
    idx = n_buf + jnp.arange(S)[:, None] - dil * jnp.arange(steps + 1)[None, :]
    valid = idx >= 0
    idx = jnp.maximum(idx, 0)
    kg = jnp.take(k_all, idx, axis=1)
    vg = jnp.take(v_all, idx, axis=1)
    s = jnp.einsum('bshd,bsjhd->bshj', q, kg)
    s = jnp.where(valid[None, :, None, :], s, -jnp.inf)
    m = jnp.max(s, axis=-1)
    p = jnp.exp(s - m[..., None])
    l = jnp.sum(p, axis=-1)
    o = jnp.einsum('bshj,bsjhd->bshd', p, vg)
    return o, m, l


def combine_branches(branches):
    ms = jnp.stack([br[1] for br in branches])
    w = jnp.exp(ms - jnp.max(ms, axis=0, keepdims=True))
    num = jnp.sum(w[..., None] * jnp.stack([br[0] for br in branches]), axis=0)
    den = jnp.sum(w * jnp.stack([br[2] for br in branches]), axis=0)
    return num / den[..., None]


def mlstm_chunkwise(q, k, v, ig, lf, C0, n0, m0):
    B, S, H, dk = q.shape
    L = MLSTM_CHUNK if S % MLSTM_CHUNK == 0 else S
    nc = S // L

    def vec_chunks(x):
        return jnp.transpose(x.reshape(B, nc, L, H, x.shape[-1]), (1, 0, 3, 2, 4))

    def gate_chunks(g):
        return jnp.transpose(g.reshape(B, nc, L, H), (1, 0, 3, 2))

    causal = jnp.tril(jnp.ones((L, L), dtype=bool))

    def step(carry, xs):
        C, n, m_prev = carry
        qc, kc, vc, ic, fc = xs
        b = jnp.cumsum(fc, axis=-1)
        d = jnp.where(causal, b[..., :, None] - b[..., None, :] + ic[..., None, :], -jnp.inf)
        m_inter = b + m_prev[..., None]
        m = jnp.maximum(m_inter, jnp.max(d, axis=-1))
        w_intra = jnp.exp(d - m[..., None]) * jnp.einsum('bhtd,bhsd->bhts', qc, kc)
        w_inter = jnp.exp(m_inter - m)
        num = w_inter[..., None] * jnp.einsum('bhtd,bhde->bhte', qc, C) + jnp.einsum('bhts,bhse->bhte', w_intra, vc)
        den = w_inter * jnp.einsum('bhtd,bhd->bht', qc, n) + jnp.sum(w_intra, axis=-1)
        h = num / jnp.maximum(jnp.abs(den), jnp.exp(-m))[..., None]
        m_end = m[..., -1]
        w_end = jnp.exp(b[..., -1:] - b + ic - m_end[..., None])
        decay = jnp.exp(b[..., -1] + m_prev - m_end)
        C_new = decay[..., None, None] * C + jnp.einsum('bhs,bhsd,bhse->bhde', w_end, kc, vc)
        n_new = decay[..., None] * n + jnp.einsum('bhs,bhsd->bhd', w_end, kc)
        return (C_new, n_new, m_end), h

    (C, n, m), hs = lax.scan(step, (C0, n0, m0),
                             (vec_chunks(q), vec_chunks(k), vec_chunks(v), gate_chunks(ig), gate_chunks(lf)))
    h = jnp.transpose(hs, (1, 0, 3, 2, 4)).reshape(B, S, H, -1)
    return h, C, n, m


def token_mix(h, pos, C0, n0, m0, k_buf, v_buf, w_in, b_gate, g_attn_out, g_mlstm_out, w_out):
    B, S, _ = h.shape
    f32 = jnp.float32
    proj = (h @ w_in).astype(f32)
    aw, mw = ATTN_WIDTH, MLSTM_WIDTH
    aq, ak, av, mq, mk, mv, mo, gates = jnp.split(
        proj, [aw, 2 * aw, 3 * aw, 3 * aw + mw, 3 * aw + 2 * mw, 3 * aw + 3 * mw, 3 * aw + 4 * mw], axis=-1)

    aq = partial_rope(aq.reshape(B, S, ATTN_HEADS, ATTN_HEAD_DIM), pos) * (ATTN_HEAD_DIM ** -0.5)
    ak = partial_rope(ak.reshape(B, S, ATTN_HEADS, ATTN_HEAD_DIM), pos)
    av = av.reshape(B, S, ATTN_HEADS, ATTN_HEAD_DIM)
    if k_buf is None:
        branches = [dilated_branch_prompt(aq, ak, av, dil, win // dil) for win, dil in DILATED_PAIRS]
        keep = min(WINDOW_MAX, S)
        k_state = ak[:, S - keep:]
        v_state = av[:, S - keep:]
    else:
        n_buf = k_buf.shape[1]
        k_all = jnp.concatenate([k_buf.astype(f32), ak], axis=1)
        v_all = jnp.concatenate([v_buf.astype(f32), av], axis=1)
        branches = [dilated_branch_sample(aq, k_all, v_all, n_buf, dil, win // dil) for win, dil in DILATED_PAIRS]
        keep = min(WINDOW_MAX, n_buf + S)
        k_state = k_all[:, n_buf + S - keep:]
        v_state = v_all[:, n_buf + S - keep:]
    attn = combine_branches(branches)

    mq = mq.reshape(B, S, MLSTM_HEADS, MLSTM_HEAD_DIM)
    mk = mk.reshape(B, S, MLSTM_HEADS, MLSTM_HEAD_DIM) * (MLSTM_HEAD_DIM ** -0.5)
    mv = mv.reshape(B, S, MLSTM_HEADS, MLSTM_HEAD_DIM)
    gates = gates + b_gate.astype(f32)
    ig = gates[..., :MLSTM_HEADS]
    lf = jax.nn.log_sigmoid(gates[..., MLSTM_HEADS:])
    hm, C, n, m = mlstm_chunkwise(mq, mk, mv, ig, lf, C0.astype(f32), n0.astype(f32), m0.astype(f32))
    hm = jax.nn.sigmoid(mo).reshape(B, S, MLSTM_HEADS, MLSTM_HEAD_DIM) * hm

    merged = jnp.concatenate([head_rmsnorm(attn, g_attn_out), head_rmsnorm(hm, g_mlstm_out)], axis=-1)
    y = merged.astype(h.dtype) @ w_out
    return y, k_state, v_state, C, n, m


def decoder_layer(x, pos, C0, n0, m0, k_buf, v_buf, norm_ffn1, w1g, w1u, w1d, norm_mix, w_in, b_gate,
                  g_attn_out, g_mlstm_out, w_out, norm_ffn2, w2g, w2u, w2d):
    x = macaron_ffn(x, norm_ffn1, w1g, w1u, w1d)
    y, k_st, v_st, C, n, m = token_mix(rmsnorm(x, norm_mix), pos, C0, n0, m0, k_buf, v_buf,
                                       w_in, b_gate, g_attn_out, g_mlstm_out, w_out)
    x = x + y
    x = macaron_ffn(x, norm_ffn2, w2g, w2u, w2d)
    return x, k_st, v_st, C, n, m


def setup_inputs(seed: int = 0) -> dict:
    key = jax.random.key(seed)
    ks = jax.random.split(key, 24)
    f32 = jnp.float32

    def nrm(k, shape, scale):
        return jax.random.normal(k, shape, f32) * scale

    def gain(k, n):
        return 1.0 + 0.01 * jax.random.normal(k, (DEPTH, n), f32)

    w_buf = min(WINDOW_MAX, PAST_LEN)
    gate_offset = jnp.concatenate([jnp.zeros((MLSTM_HEADS,), f32), jnp.full((MLSTM_HEADS,), FORGET_BIAS, f32)])
    return {
        'x_prompt': nrm(ks[0], (BATCH, SEQ, D_MODEL), 1.0),
        'x_sample': nrm(ks[1], (DEC_BATCH, DEC_SEQ, D_MODEL), 1.0),
        'cache_k': nrm(ks[2], (DEPTH, DEC_BATCH, w_buf, ATTN_HEADS, ATTN_HEAD_DIM), 1.0),
        'cache_v': nrm(ks[3], (DEPTH, DEC_BATCH, w_buf, ATTN_HEADS, ATTN_HEAD_DIM), 1.0),
        'state_C': nrm(ks[4], (DEPTH, DEC_BATCH, MLSTM_HEADS, MLSTM_HEAD_DIM, MLSTM_HEAD_DIM), 0.1),
        'state_n': nrm(ks[5], (DEPTH, DEC_BATCH, MLSTM_HEADS, MLSTM_HEAD_DIM), 0.1),
        'state_m': nrm(ks[6], (DEPTH, DEC_BATCH, MLSTM_HEADS), 0.5),
        'norm_ffn1': gain(ks[7], D_MODEL),
        'w_ffn1_gate': nrm(ks[8], (DEPTH, D_MODEL, D_FF), D_MODEL ** -0.5),
        'w_ffn1_up': nrm(ks[9], (DEPTH, D_MODEL, D_FF), D_MODEL ** -0.5),
        'w_ffn1_down': nrm(ks[10], (DEPTH, D_FF, D_MODEL), D_FF ** -0.5),
        'norm_mix': gain(ks[11], D_MODEL),
        'w_in': nrm(ks[12], (DEPTH, D_MODEL, IN_COLS), D_MODEL ** -0.5),
        'b_gate': nrm(ks[13], (DEPTH, 2 * MLSTM_HEADS), 0.1) + gate_offset,
        'g_attn_out': gain(ks[14], ATTN_WIDTH),
        'g_mlstm_out': gain(ks[15], MLSTM_WIDTH),
        'w_out': nrm(ks[16], (DEPTH, MIX_WIDTH, D_MODEL), MIX_WIDTH ** -0.5),
        'norm_ffn2': gain(ks[17], D_MODEL),
        'w_ffn2_gate': nrm(ks[18], (DEPTH, D_MODEL, D_FF), D_MODEL ** -0.5),
        'w_ffn2_up': nrm(ks[19], (DEPTH, D_MODEL, D_FF), D_MODEL ** -0.5),
        'w_ffn2_down': nrm(ks[20], (DEPTH, D_FF, D_MODEL), D_FF ** -0.5),
        'norm_final': 1.0 + 0.01 * jax.random.normal(ks[21], (D_MODEL,), f32),
    }


def reference(x_prompt, x_sample, cache_k, cache_v, state_C, state_n, state_m, norm_ffn1, w_ffn1_gate,
              w_ffn1_up, w_ffn1_down, norm_mix, w_in, b_gate, g_attn_out, g_mlstm_out, w_out, norm_ffn2,
              w_ffn2_gate, w_ffn2_up, w_ffn2_down, norm_final):
    f32 = jnp.float32
    B, S, _ = x_prompt.shape
    DB, DS, _ = x_sample.shape
    pos_p = jnp.arange(S, dtype=jnp.int32)
    pos_s = PAST_LEN + jnp.arange(DS, dtype=jnp.int32)
    xp, xs = x_prompt, x_sample
    st_p = [[], [], [], [], []]
    st_s = [[], [], [], [], []]
    for l in range(DEPTH):
        params = (norm_ffn1[l], w_ffn1_gate[l], w_ffn1_up[l], w_ffn1_down[l], norm_mix[l], w_in[l], b_gate[l],
                  g_attn_out[l], g_mlstm_out[l], w_out[l], norm_ffn2[l], w_ffn2_gate[l], w_ffn2_up[l],
                  w_ffn2_down[l])
        C0 = jnp.zeros((B, MLSTM_HEADS, MLSTM_HEAD_DIM, MLSTM_HEAD_DIM), f32)
        n0 = jnp.zeros((B, MLSTM_HEADS, MLSTM_HEAD_DIM), f32)
        m0 = jnp.zeros((B, MLSTM_HEADS), f32)
        xp, kp, vp, Cp, npr, mp = decoder_layer(xp, pos_p, C0, n0, m0, None, None, *params)
        xs, ksn, vsn, Cs, nsn, msn = decoder_layer(xs, pos_s, state_C[l], state_n[l], state_m[l],
                                                  cache_k[l], cache_v[l], *params)
        for lst, val in zip(st_p, (kp, vp, Cp, npr, mp)):
            lst.append(val)
        for lst, val in zip(st_s, (ksn, vsn, Cs, nsn, msn)):
            lst.append(val)
    y_prompt = rmsnorm(xp, norm_final)
    y_sample = rmsnorm(xs, norm_final)
    return (y_prompt, y_sample,
            jnp.stack(st_p[0]), jnp.stack(st_p[1]), jnp.stack(st_p[2]), jnp.stack(st_p[3]), jnp.stack(st_p[4]),
            jnp.stack(st_s[0]), jnp.stack(st_s[1]), jnp.stack(st_s[2]), jnp.stack(st_s[3]), jnp.stack(st_s[4]))
```

```python
import functools
import math

import jax
import jax.numpy as jnp
import numpy as np
from jax import lax
from jax.experimental import pallas as pl
from jax.experimental.pallas import tpu as pltpu

F32 = jnp.float32
BF16 = jnp.bfloat16

EPS = 1e-6
ROPE_THETA = 500000.0
ATTN_HEAD_DIM = 64
ROT_DIM = ATTN_HEAD_DIM // 4
MLSTM_HEADS = 4
MLSTM_HEAD_DIM = 128
DILATIONS = (16, 4, 1)
STEPS = 128
WINDOW_MAX = 2048
PAST_LEN = 8192
FORGET_COL = MLSTM_HEADS

LANES = 128
FF_CHUNK = 256
VMEM_LIMIT = 56 * 1024 * 1024


def _cparams(sem):
    return pltpu.CompilerParams(dimension_semantics=sem, vmem_limit_bytes=VMEM_LIMIT)


def _const_spec(shape):
    nd = len(shape)
    return pl.BlockSpec(shape, lambda *_: (0,) * nd, pipeline_mode=pl.Buffered(1))


def _rms(x, g):
    return x * lax.rsqrt(jnp.mean(x * x, axis=-1, keepdims=True) + EPS) * g


def _swiglu_ffn(x, g_ref, wg_ref, wu_ref, wd_ref, acc_ref):
    h = _rms(x, g_ref[...]).astype(BF16)
    acc_ref[...] = jnp.zeros_like(acc_ref)

    def chunk(c, carry):
        gt = jnp.dot(h, wg_ref[c], preferred_element_type=F32)
        ut = jnp.dot(h, wu_ref[c], preferred_element_type=F32)
        a = (gt * jax.nn.sigmoid(gt) * ut).astype(BF16)
        acc_ref[...] += jnp.dot(a, wd_ref[c], preferred_element_type=F32)
        return carry

    lax.fori_loop(0, wg_ref.shape[0], chunk, 0)
    return x + 0.5 * acc_ref[...]


def _ffn_kernel(x_ref, g_ref, wg_ref, wu_ref, wd_ref, o_ref, acc_ref):
    o_ref[...] = _swiglu_ffn(x_ref[...], g_ref, wg_ref, wu_ref, wd_ref, acc_ref)


def _ffn(x, g, wg, wu, wd, tm):
    m, d = x.shape
    row = pl.BlockSpec((tm, d), lambda i: (i, 0))
    return pl.pallas_call(
        _ffn_kernel,
        grid=(m // tm,),
        in_specs=[row, _const_spec(g.shape), _const_spec(wg.shape), _const_spec(wu.shape),
                  _const_spec(wd.shape)],
        out_specs=row,
        out_shape=jax.ShapeDtypeStruct((m, d), F32),
        scratch_shapes=[pltpu.VMEM((tm, d), F32)],
        compiler_params=_cparams(("parallel",)),
        name="ffn",
    )(x, g, wg, wu, wd)


def _rope(y, cos, s_up, s_dn):
    parts = []
    for i in range(y.shape[1] // LANES):
        ys = y[:, LANES * i:LANES * (i + 1)]
        parts.append(ys * cos + pltpu.roll(ys, LANES - ROT_DIM // 2, 1) * s_up
                     + pltpu.roll(ys, ROT_DIM // 2, 1) * s_dn)
    return jnp.concatenate(parts, axis=1)


def _inproj_kernel(x_ref, g_ref, wq_ref, wk_ref, wv_ref, wmq_ref, wmk_ref, wmv_ref, wmo_ref, wgt_ref,
                   bg_ref, cos_ref, sup_ref, sdn_ref,
                   q_ref, k_ref, v_ref, mq_ref, mk_ref, mv_ref, mo_ref, gt_ref):
    h = _rms(x_ref[...], g_ref[...]).astype(BF16)
    cos, s_up, s_dn = cos_ref[...], sup_ref[...], sdn_ref[...]

    def proj(w_ref):
        return jnp.dot(h, w_ref[...], preferred_element_type=F32)

    q_ref[...] = _rope(proj(wq_ref), cos, s_up, s_dn) * (ATTN_HEAD_DIM ** -0.5)
    k_ref[...] = _rope(proj(wk_ref), cos, s_up, s_dn)
    v_ref[...] = proj(wv_ref)
    mq_ref[...] = proj(wmq_ref).astype(BF16)
    mk_ref[...] = (proj(wmk_ref) * (MLSTM_HEAD_DIM ** -0.5)).astype(BF16)
    mv_ref[...] = proj(wmv_ref).astype(BF16)
    mo_ref[...] = proj(wmo_ref)
    gt_ref[...] = proj(wgt_ref) + bg_ref[...]


def _inproj(x, g, ws, wgt, bg, tabs, tm):
    m, d = x.shape
    w = ws[0].shape[1]
    row = lambda width: pl.BlockSpec((tm, width), lambda i: (i, 0))
    nt = tabs[0].shape[0] // tm
    tab = pl.BlockSpec((tm, LANES), lambda i: (i % nt, 0))
    out_dt = (F32, F32, F32, BF16, BF16, BF16, F32)
    return pl.pallas_call(
        _inproj_kernel,
        grid=(m // tm,),
        in_specs=[row(d), _const_spec(g.shape)] + [_const_spec(wi.shape) for wi in ws]
                 + [_const_spec(wgt.shape), _const_spec(bg.shape), tab, tab, tab],
        out_specs=[row(w)] * 7 + [row(LANES)],
        out_shape=[jax.ShapeDtypeStruct((m, w), dt) for dt in out_dt]
                  + [jax.ShapeDtypeStruct((m, LANES), F32)],
        compiler_params=_cparams(("parallel",)),
        name="inproj",
    )(x, g, *ws, wgt, bg, *tabs)


def _attn_kernel(q_ref, k_ref, v_ref, g_ref, o_ref, acc_ref, m_ref, l_ref):
    s_len = q_ref.shape[0]
    blk = STEPS
    lane = lax.broadcasted_iota(jnp.int32, (1, LANES), 1)
    head0 = lane < ATTN_HEAD_DIM
    qi = lax.broadcasted_iota(jnp.int32, (2 * blk, 2 * blk), 0) & (blk - 1)
    kj = lax.broadcasted_iota(jnp.int32, (2 * blk, 2 * blk), 1)
    band = (kj >= qi) & (kj <= qi + blk)
    cur_half = kj >= blk

    def pick(a):
        return jnp.where(head0, a[:blk], a[blk:])

    def branch(dil, first, last):
        nblk = s_len // (dil * blk)

        def unit(u, carry):
            r = u // nblk
            n = u % nblk
            qstart = r + dil * blk * n
            pstart = r + dil * blk * jnp.maximum(n - 1, 0)
            if dil == 1:
                qstart, pstart = pl.multiple_of(qstart, blk), pl.multiple_of(pstart, blk)
            rows = lambda start: pl.ds(start, blk, stride=dil) if dil > 1 else pl.ds(start, blk)
            q = q_ref[rows(qstart), :]
            kk = jnp.concatenate([k_ref[rows(pstart), :], k_ref[rows(qstart), :]], axis=0).astype(BF16)
            vv = jnp.concatenate([v_ref[rows(pstart), :], v_ref[rows(qstart), :]], axis=0).astype(BF16)
            qs = jnp.concatenate([jnp.where(head0, q, 0.0), jnp.where(head0, 0.0, q)], axis=0).astype(BF16)
            s = lax.dot_general(qs, kk, (((1,), (1,)), ((), ())), preferred_element_type=F32)
            no_prev = jnp.where(n > 0, 0.0, -jnp.inf)
            s = jnp.where(band, jnp.where(cur_half, s, s + no_prev), -jnp.inf)
            mb = jnp.max(s, axis=1, keepdims=True)
            p = jnp.exp(s - mb)
            lb = jnp.sum(p, axis=1, keepdims=True)
            ob = pick(jnp.dot(p.astype(BF16), vv, preferred_element_type=F32))
            mb = pick(jnp.broadcast_to(mb, (2 * blk, LANES)))
            lb = pick(jnp.broadcast_to(lb, (2 * blk, LANES)))
            if first:
                m_new, l_new, acc = mb, lb, ob
            else:
                m_old = m_ref[rows(qstart), :]
                m_new = jnp.maximum(m_old, mb)
                w_old = jnp.exp(m_old - m_new)
                w_blk = jnp.exp(mb - m_new)
                l_new = l_ref[rows(qstart), :] * w_old + lb * w_blk
                acc = acc_ref[rows(qstart), :] * w_old + ob * w_blk
            if last:
                y = acc / l_new
                sq = y * y
                ms0 = jnp.sum(jnp.where(head0, sq, 0.0), axis=1, keepdims=True)
                ms1 = jnp.sum(jnp.where(head0, 0.0, sq), axis=1, keepdims=True)
                ms = jnp.where(head0, ms0, ms1) * (1.0 / ATTN_HEAD_DIM)
                o_ref[rows(qstart), :] = (y * lax.rsqrt(ms + EPS) * g_ref[...]).astype(o_ref.dtype)
            else:
                m_ref[rows(qstart), :] = m_new
                l_ref[rows(qstart), :] = l_new
                acc_ref[rows(qstart), :] = acc
            return carry

        lax.fori_loop(0, dil * nblk, unit, 0)

    for idx, dil in enumerate(DILATIONS):
        branch(dil, idx == 0, idx == len(DILATIONS) - 1)


def _attention(q, k, v, g, batch):
    m, w = q.shape
    s_len = m // batch
    assert s_len % (max(DILATIONS) * STEPS) == 0
    nslab = w // LANES
    blk = pl.BlockSpec((s_len, LANES), lambda b, j: (b, j))
    return pl.pallas_call(
        _attn_kernel,
        grid=(batch, nslab),
        in_specs=[blk, blk, blk, pl.BlockSpec((1, LANES), lambda b, j: (0, j))],
        out_specs=blk,
        out_shape=jax.ShapeDtypeStruct((m, w), BF16),
        scratch_shapes=[pltpu.VMEM((s_len, LANES), F32)] * 3,
        compiler_params=_cparams(("parallel", "parallel")),
        name="attention",
    )(q, k, v, g)


def _log_sigmoid(x):
    return jnp.minimum(x, 0.0) - jnp.log1p(jnp.exp(-jnp.abs(x)))


def _head_out(h, mo, g):
    hm = jax.nn.sigmoid(mo) * h
    return hm * lax.rsqrt(jnp.mean(hm * hm, axis=-1, keepdims=True) + EPS) * g


def _mlstm_kernel(q_ref, k_ref, v_ref, mo_ref, gt_ref, g_ref, o_ref, c_out, n_out, m_out,
                  c_sc, n_sc, m_sc):
    chunk = pl.program_id(1)
    length = q_ref.shape[0]
    hd = MLSTM_HEAD_DIM

    @pl.when(chunk == 0)
    def _():
        c_sc[...] = jnp.zeros_like(c_sc)
        n_sc[...] = jnp.zeros_like(n_sc)
        m_sc[...] = jnp.zeros_like(m_sc)

    gates = gt_ref[...]
    lf = _log_sigmoid(gates)
    ti = lax.broadcasted_iota(jnp.int32, (length, length), 0)
    si = lax.broadcasted_iota(jnp.int32, (length, length), 1)
    causal = si <= ti
    tril = causal.astype(BF16)
    hi = lf.astype(BF16)
    rem = lf - hi.astype(F32)
    mid = rem.astype(BF16)
    lo = (rem - mid.astype(F32)).astype(BF16)
    bcum = (jnp.dot(tril, hi, preferred_element_type=F32) + jnp.dot(tril, mid, preferred_element_type=F32)
            + jnp.dot(tril, lo, preferred_element_type=F32))
    gates_t = gates.T
    bcum_t = bcum.T

    for h in range(MLSTM_HEADS):
        f = FORGET_COL + h
        sl = slice(hd * h, hd * (h + 1))
        q, k, v = q_ref[:, sl], k_ref[:, sl], v_ref[:, sl]
        b_col = bcum[:, f:f + 1]
        ig_col = gates[:, h:h + 1]
        b_row = bcum_t[f:f + 1, :]
        ig_row = gates_t[h:h + 1, :]
        m_prev = m_sc[h][0:1, 0:1]
        c_prev = c_sc[h]
        n_prev = n_sc[h][0:1, :]

        d = jnp.where(causal, b_col - b_row + ig_row, -jnp.inf)
        m_inter = b_col + m_prev
        m = jnp.maximum(m_inter, jnp.max(d, axis=1, keepdims=True))
        qk = lax.dot_general(q, k, (((1,), (1,)), ((), ())), preferred_element_type=F32)
        w_intra = jnp.exp(d - m) * qk
        w_inter = jnp.exp(m_inter - m)
        num = (w_inter * jnp.dot(q, c_prev.astype(BF16), preferred_element_type=F32)
               + jnp.dot(w_intra.astype(BF16), v, preferred_element_type=F32))
        den = (w_inter * jnp.sum(q.astype(F32) * n_prev, axis=1, keepdims=True)
               + jnp.sum(w_intra, axis=1, keepdims=True))
        hh = num / jnp.maximum(jnp.abs(den), jnp.exp(-m))
        o_ref[:, sl] = _head_out(hh, mo_ref[:, sl], g_ref[:, sl]).astype(o_ref.dtype)

        m_end = m[length - 1:length, :]
        b_end = b_col[length - 1:length, :]
        w_end = jnp.exp(b_end - b_col + ig_col - m_end)
        decay = jnp.exp(b_end + m_prev - m_end)
        kw = k.astype(F32) * w_end
        c_new = decay * c_prev + lax.dot_general(kw.astype(BF16), v, (((0,), (0,)), ((), ())),
                                                 preferred_element_type=F32)
        n_new = decay * n_prev + jnp.sum(kw, axis=0, keepdims=True)
        c_sc[h] = c_new
        n_sc[h] = jnp.broadcast_to(n_new, n_sc.shape[1:])
        m_sc[h] = jnp.broadcast_to(m_end, m_sc.shape[1:])

    @pl.when(chunk == pl.num_programs(1) - 1)
    def _():
        c_out[0] = c_sc[...]
        n_out[0] = n_sc[...][:, 0, :]
        m_out[0] = m_sc[...][:, 0, :]


def _mlstm(mq, mk, mv, mo, gates, g, batch, chunk_len):
    m, w = mq.shape
    s_len = m // batch
    nck = s_len // chunk_len
    nh, hd = MLSTM_HEADS, MLSTM_HEAD_DIM
    row = lambda width: pl.BlockSpec((chunk_len, width), lambda b, c: (b * nck + c, 0))
    return pl.pallas_call(
        _mlstm_kernel,
        grid=(batch, nck),
        in_specs=[row(w), row(w), row(w), row(w), row(LANES), pl.BlockSpec((1, w), lambda b, c: (0, 0))],
        out_specs=[row(w),
                   pl.BlockSpec((1, nh, hd, hd), lambda b, c: (b, 0, 0, 0)),
                   pl.BlockSpec((1, nh, hd), lambda b, c: (b, 0, 0)),
                   pl.BlockSpec((1, nh, LANES), lambda b, c: (b, 0, 0))],
        out_shape=[jax.ShapeDtypeStruct((m, w), BF16),
                   jax.ShapeDtypeStruct((batch, nh, hd, hd), F32),
                   jax.ShapeDtypeStruct((batch, nh, hd), F32),
                   jax.ShapeDtypeStruct((batch, nh, LANES), F32)],
        scratch_shapes=[pltpu.VMEM((nh, hd, hd), F32), pltpu.VMEM((nh, 8, hd), F32),
                        pltpu.VMEM((nh, 8, LANES), F32)],
        compiler_params=_cparams(("parallel", "arbitrary")),
        name="mlstm",
    )(mq, mk, mv, mo, gates, g)


def _out_ffn_kernel(x_ref, a_ref, mm_ref, woa_ref, wom_ref, g_ref, wg_ref, wu_ref, wd_ref, gf_ref,
                    o_ref, acc_ref):
    x = (x_ref[...] + jnp.dot(a_ref[...], woa_ref[...], preferred_element_type=F32)
         + jnp.dot(mm_ref[...], wom_ref[...], preferred_element_type=F32))
    x = _swiglu_ffn(x, g_ref, wg_ref, wu_ref, wd_ref, acc_ref)
    o_ref[...] = _rms(x, gf_ref[...])


def _out_ffn(x, a, mm, woa, wom, g, wg, wu, wd, gf, tm):
    m, d = x.shape
    row = lambda width: pl.BlockSpec((tm, width), lambda i: (i, 0))
    consts = (woa, wom, g, wg, wu, wd, gf)
    return pl.pallas_call(
        _out_ffn_kernel,
        grid=(m // tm,),
        in_specs=[row(d), row(a.shape[1]), row(mm.shape[1])] + [_const_spec(c.shape) for c in consts],
        out_specs=row(d),
        out_shape=jax.ShapeDtypeStruct((m, d), F32),
        scratch_shapes=[pltpu.VMEM((tm, d), F32)],
        compiler_params=_cparams(("parallel",)),
        name="out_ffn",
    )(x, a, mm, *consts)


def _sample_attn_kernel(q_ref, kn_ref, vn_ref, g_ref, *refs):
    nb = len(DILATIONS)
    k_refs, v_refs, o_ref = refs[:nb], refs[nb:2 * nb], refs[2 * nb]
    q = q_ref[...]
    kn, vn = kn_ref[...], vn_ref[...]
    s_self = jnp.sum(q * kn, axis=-1, keepdims=True)
    ms, ls, os_ = [], [], []
    for kc_ref, vc_ref in zip(k_refs, v_refs):
        s = jnp.sum(kc_ref[...] * q[None], axis=-1, keepdims=True)
        mb = jnp.maximum(jnp.max(s, axis=0), s_self)
        p = jnp.exp(s - mb[None])
        p_self = jnp.exp(s_self - mb)
        ms.append(mb)
        ls.append(jnp.sum(p, axis=0) + p_self)
        os_.append(jnp.sum(p * vc_ref[...], axis=0) + p_self * vn)
    m_max = functools.reduce(jnp.maximum, ms)
    ws = [jnp.exp(mb - m_max) for mb in ms]
    num = sum(wb * ob for wb, ob in zip(ws, os_))
    den = sum(wb * lb for wb, lb in zip(ws, ls))
    y = num / den
    y = y * lax.rsqrt(jnp.mean(y * y, axis=-1, keepdims=True) + EPS) * g_ref[...]
    o_ref[...] = y.astype(o_ref.dtype)


def _sample_attention(q3, kn3, vn3, g2, cache_k, cache_v):
    b, nh, hd = q3.shape
    w = cache_k.shape[1]
    tok = pl.BlockSpec((None, nh, hd), lambda i: (i, 0, 0))
    views, specs = [], []
    for cache in (cache_k, cache_v):
        for dil in DILATIONS:
            views.append(cache.reshape(b, w // dil, dil, nh, hd))
            specs.append(pl.BlockSpec((None, STEPS, None, nh, hd),
                                      lambda i, _n=w // dil // STEPS - 1: (i, _n, 0, 0, 0)))
    return pl.pallas_call(
        _sample_attn_kernel,
        grid=(b,),
        in_specs=[tok, tok, tok, pl.BlockSpec((nh, hd), lambda i: (0, 0))] + specs,
        out_specs=tok,
        out_shape=jax.ShapeDtypeStruct((b, nh, hd), BF16),
        compiler_params=_cparams(("parallel",)),
        name="sample_attention",
    )(q3, kn3, vn3, g2, *views)


def _shift_kernel(ck_ref, cv_ref, nk_ref, nv_ref, newk_ref, newv_ref, ok_ref, ov_ref):
    p = ck_ref.shape[0]
    is_last = pl.program_id(1) == pl.num_programs(1) - 1
    for c_ref, nxt_ref, new_ref, o_ref in ((ck_ref, nk_ref, newk_ref, ok_ref),
                                           (cv_ref, nv_ref, newv_ref, ov_ref)):
        o_ref[0:p - 1] = c_ref[1:p]
        o_ref[p - 1] = jnp.where(is_last, new_ref[...], nxt_ref[0])


def _cache_shift(cache_k, cache_v, kn3, vn3, rows):
    b, w, nh, hd = cache_k.shape
    blk = pl.BlockSpec((None, rows, nh, hd), lambda i, j: (i, j, 0, 0))
    nxt = pl.BlockSpec((None, 1, nh, hd), lambda i, j: (i, jnp.minimum((j + 1) * rows, w - 1), 0, 0))
    new = pl.BlockSpec((None, nh, hd), lambda i, j: (i, 0, 0))
    return pl.pallas_call(
        _shift_kernel,
        grid=(b, w // rows),
        in_specs=[blk, blk, nxt, nxt, new, new],
        out_specs=[blk, blk],
        out_shape=[jax.ShapeDtypeStruct(cache_k.shape, F32)] * 2,
        compiler_params=_cparams(("parallel", "parallel")),
        name="cache_shift",
    )(cache_k, cache_v, cache_k, cache_v, kn3, vn3)


def _mlstm_step_kernel(q_ref, k_ref, v_ref, mo_ref, gt_ref, g_ref, c_ref, n_ref, m_ref,
                       o_ref, c_out, n_out, m_out):
    bb = q_ref.shape[0]
    hd = MLSTM_HEAD_DIM
    gates = gt_ref[...]
    lane = lax.broadcasted_iota(jnp.int32, (bb, LANES), 1)
    eye = (lax.broadcasted_iota(jnp.int32, (hd, hd), 0)
           == lax.broadcasted_iota(jnp.int32, (hd, hd), 1)).astype(BF16)
    m_all = jnp.zeros((bb, LANES), F32)
    for h in range(MLSTM_HEADS):
        sl = slice(hd * h, hd * (h + 1))
        q, k, v = q_ref[:, sl], k_ref[:, sl], v_ref[:, sl]
        qf, kf, vf = q.astype(F32), k.astype(F32), v.astype(F32)
        ig = gates[:, h:h + 1]
        lf = _log_sigmoid(gates[:, FORGET_COL + h:FORGET_COL + h + 1])
        m_prev = m_ref[:, h:h + 1]
        n_prev = n_ref[:, sl]
        m_inter = lf + m_prev
        m = jnp.maximum(m_inter, ig)
        w_in = jnp.exp(ig - m)
        w_inter = jnp.exp(m_inter - m)
        q_t = lax.dot_general(eye, q, (((1,), (1,)), ((), ())), preferred_element_type=F32)
        k_t = lax.dot_general(eye, k, (((1,), (1,)), ((), ())), preferred_element_type=F32)
        qc_rows = []
        for i in range(bb):
            c_prev = c_ref[0, i, h]
            c_bf = c_prev.astype(BF16).astype(F32)
            qc_rows.append(jnp.sum(q_t[:, i:i + 1] * c_bf, axis=0, keepdims=True))
            c_out[0, i, h] = (w_inter[i:i + 1, :] * c_prev
                              + (w_in[i:i + 1, :] * k_t[:, i:i + 1]) * vf[i:i + 1, :])
        qc = jnp.concatenate(qc_rows, axis=0)
        w_intra = w_in * jnp.sum(qf * kf, axis=1, keepdims=True)
        num = w_inter * qc + w_intra * vf
        den = w_inter * jnp.sum(qf * n_prev, axis=1, keepdims=True) + w_intra
        hh = num / jnp.maximum(jnp.abs(den), jnp.exp(-m))
        o_ref[:, sl] = _head_out(hh, mo_ref[:, sl], g_ref[:, sl]).astype(o_ref.dtype)
        n_out[:, sl] = w_inter * n_prev + w_in * kf
        m_all = jnp.where(lane == h, m, m_all)
    m_out[...] = m_all


def _mlstm_step(mq, mk, mv, mo, gates, g, state_c, state_n, state_m, bb):
    b, w = mq.shape
    nh, hd = MLSTM_HEADS, MLSTM_HEAD_DIM
    row = lambda width: pl.BlockSpec((bb, width), lambda i: (i, 0))
    c_spec = pl.BlockSpec((1, bb, nh, hd, hd), lambda i: (0, i, 0, 0, 0))
    return pl.pallas_call(
        _mlstm_step_kernel,
        grid=(b // bb,),
        in_specs=[row(w), row(w), row(w), row(w), row(LANES), pl.BlockSpec((1, w), lambda i: (0, 0)),
                  c_spec, row(w), row(nh)],
        out_specs=[row(w), c_spec, row(w), row(LANES)],
        out_shape=[jax.ShapeDtypeStruct((b, w), BF16),
                   jax.ShapeDtypeStruct(state_c.shape, F32),
                   jax.ShapeDtypeStruct((b, w), F32),
                   jax.ShapeDtypeStruct((b, LANES), F32)],
        compiler_params=_cparams(("parallel",)),
        name="mlstm_step",
    )(mq, mk, mv, mo, gates, g, state_c, state_n, state_m)


def _rope_tables(pos):
    half = ROT_DIM // 2
    inv = jnp.power(jnp.float32(ROPE_THETA), -jnp.arange(half, dtype=F32) * 2.0 / ROT_DIM)
    ang = pos.astype(F32)[:, None] * inv[None, :]
    cos, sin = jnp.cos(ang), jnp.sin(ang)
    rest = ATTN_HEAD_DIM - ROT_DIM
    one = jnp.ones((pos.shape[0], rest), F32)
    zero = jnp.zeros((pos.shape[0], rest), F32)
    zh = jnp.zeros_like(sin)
    head = lambda parts: jnp.tile(jnp.concatenate(parts, axis=1), (1, LANES // ATTN_HEAD_DIM))
    return head([cos, cos, one]), head([-sin, zh, zero]), head([zh, sin, zero])


def _ffn_weights(g, w_gate, w_up, w_down):
    d, f = w_gate.shape
    nck = f // FF_CHUNK
    cols = lambda w: jnp.transpose(w.astype(BF16).reshape(d, nck, FF_CHUNK), (1, 0, 2))
    return g.reshape(1, d), cols(w_gate), cols(w_up), w_down.astype(BF16).reshape(nck, FF_CHUNK, d)


def _row_tile(m, want):
    return want if m % want == 0 else m


def kernel(x_prompt, x_sample, cache_k, cache_v, state_C, state_n, state_m, norm_ffn1, w_ffn1_gate,
           w_ffn1_up, w_ffn1_down, norm_mix, w_in, b_gate, g_attn_out, g_mlstm_out, w_out, norm_ffn2,
           w_ffn2_gate, w_ffn2_up, w_ffn2_down, norm_final):
    batch, s_len, d = x_prompt.shape
    db, ds, _ = x_sample.shape
    assert norm_ffn1.shape[0] == 1 and ds == 1
    aw = g_attn_out.shape[1]
    mw = g_mlstm_out.shape[1]
    nh_attn = aw // ATTN_HEAD_DIM
    w_buf = cache_k.shape[2]
    assert w_buf == WINDOW_MAX and w_buf % (max(DILATIONS) * STEPS) == 0

    ffn1 = _ffn_weights(norm_ffn1[0], w_ffn1_gate[0], w_ffn1_up[0], w_ffn1_down[0])
    ffn2 = _ffn_weights(norm_ffn2[0], w_ffn2_gate[0], w_ffn2_up[0], w_ffn2_down[0])
    w_in_bf = w_in[0].astype(BF16)
    ws = [w_in_bf[:, i * aw:(i + 1) * aw] for i in range(7)]
    ngate = 2 * MLSTM_HEADS
    wgt = jnp.pad(w_in_bf[:, 7 * aw:], ((0, 0), (0, LANES - ngate)))
    bg = jnp.pad(b_gate[0], (0, LANES - ngate)).reshape(1, LANES)
    g_mix = norm_mix.reshape(1, d)
    g_attn = g_attn_out.reshape(1, aw)
    g_ml = g_mlstm_out.reshape(1, mw)
    wo = w_out[0].astype(BF16)
    woa, wom = wo[:aw], wo[aw:]
    g_fin = norm_final.reshape(1, d)

    def trunk_in(x, pos, tm):
        x1 = _ffn(x, *ffn1, tm=tm)
        proj = _inproj(x1, g_mix, ws, wgt, bg, _rope_tables(pos), tm=tm)
        return x1, proj

    def trunk_out(x1, a, mm, tm):
        return _out_ffn(x1, a, mm, woa, wom, *ffn2, g_fin, tm=tm)

    mp = batch * s_len
    tm_p = _row_tile(s_len, 512)
    x1, (q, k, v, mq, mk, mv, mo, gates) = trunk_in(x_prompt.reshape(mp, d), jnp.arange(s_len), tm_p)
    a = _attention(q, k, v, g_attn, batch)
    mm, c_p, n_p, m_p = _mlstm(mq, mk, mv, mo, gates, g_ml, batch, _row_tile(s_len, 256))
    y_prompt = trunk_out(x1, a, mm, tm_p).reshape(batch, s_len, d)
    keep = min(WINDOW_MAX, s_len)
    k_prompt = k.reshape(batch, s_len, nh_attn, ATTN_HEAD_DIM)[:, s_len - keep:]
    v_prompt = v.reshape(batch, s_len, nh_attn, ATTN_HEAD_DIM)[:, s_len - keep:]

    pos_s = jnp.full((db,), PAST_LEN, jnp.int32)
    x1s, (qs, ks, vs, mqs, mks, mvs, mos, gts) = trunk_in(x_sample.reshape(db, d), pos_s, db)
    to3 = lambda t: t.reshape(db, nh_attn, ATTN_HEAD_DIM)
    q3, kn3, vn3 = to3(qs), to3(ks), to3(vs)
    a_s = _sample_attention(q3, kn3, vn3, g_attn.reshape(nh_attn, ATTN_HEAD_DIM), cache_k[0], cache_v[0])
    k_s, v_s = _cache_shift(cache_k[0], cache_v[0], kn3, vn3, rows=512)
    mm_s, c_s, n_s, m_s = _mlstm_step(mqs, mks, mvs, mos, gts, g_ml, state_C, state_n.reshape(db, mw),
                                      state_m[0], bb=16)
    n_s = n_s.reshape(state_n.shape)
    y_sample = trunk_out(x1s, a_s.reshape(db, aw), mm_s, db).reshape(db, 1, d)

    return (y_prompt, y_sample,
            k_prompt[None], v_prompt[None], c_p[None], n_p[None], m_p[None, :, :, 0],
            k_s[None], v_s[None], c_s, n_s, m_s[None, :, :MLSTM_HEADS])
```

```python
import functools
import math

import jax
import jax.numpy as jnp
import numpy as np
from jax import lax
from jax.experimental import pallas as pl
from jax.experimental.pallas import tpu as pltpu

F32 = jnp.float32
BF16 = jnp.bfloat16

EPS = 1e-6
ROPE_THETA = 500000.0
ATTN_HEAD_DIM = 64
ROT_DIM = ATTN_HEAD_DIM // 4
MLSTM_HEADS = 4
MLSTM_HEAD_DIM = 128
DILATIONS = (16, 4, 1)
STEPS = 128
WINDOW_MAX = 2048
PAST_LEN = 8192
FORGET_COL = MLSTM_HEADS

LANES = 128
FF_CHUNK = 256
ATTN_GROUP = 4
VMEM_LIMIT = 56 * 1024 * 1024


def _cparams(sem):
    return pltpu.CompilerParams(dimension_semantics=sem, vmem_limit_bytes=VMEM_LIMIT)


def _const_spec(shape):
    nd = len(shape)
    return pl.BlockSpec(shape, lambda *_: (0,) * nd, pipeline_mode=pl.Buffered(1))


def _rms(x, g):
    return x * lax.rsqrt(jnp.mean(x * x, axis=-1, keepdims=True) + EPS) * g


def _swiglu_ffn(x, g_ref, wg_ref, wu_ref, wd_ref, acc_ref):
    h = _rms(x, g_ref[...]).astype(BF16)
    for c in range(wg_ref.shape[1] // FF_CHUNK):
        cols = slice(c * FF_CHUNK, (c + 1) * FF_CHUNK)
        gt = jnp.dot(h, wg_ref[:, cols], preferred_element_type=F32)
        ut = jnp.dot(h, wu_ref[:, cols], preferred_element_type=F32)
        a = (gt * jax.nn.sigmoid(gt) * ut).astype(BF16)
        part = jnp.dot(a, wd_ref[cols, :], preferred_element_type=F32)
        if c == 0:
            acc_ref[...] = part
        else:
            acc_ref[...] += part
    return x + 0.5 * acc_ref[...]


def _ffn_kernel(x_ref, g_ref, wg_ref, wu_ref, wd_ref, o_ref, acc_ref):
    o_ref[...] = _swiglu_ffn(x_ref[...], g_ref, wg_ref, wu_ref, wd_ref, acc_ref)


def _ffn(x, g, wg, wu, wd, tm):
    m, d = x.shape
    row = pl.BlockSpec((tm, d), lambda i: (i, 0))
    return pl.pallas_call(
        _ffn_kernel,
        grid=(m // tm,),
        in_specs=[row, _const_spec(g.shape), _const_spec(wg.shape), _const_spec(wu.shape),
                  _const_spec(wd.shape)],
        out_specs=row,
        out_shape=jax.ShapeDtypeStruct((m, d), F32),
        scratch_shapes=[pltpu.VMEM((tm, d), F32)],
        compiler_params=_cparams(("parallel",)),
        name="ffn",
    )(x, g, wg, wu, wd)


def _rope(y, cos, s_up, s_dn):
    parts = []
    for i in range(y.shape[1] // LANES):
        ys = y[:, LANES * i:LANES * (i + 1)]
        parts.append(ys * cos + pltpu.roll(ys, LANES - ROT_DIM // 2, 1) * s_up
                     + pltpu.roll(ys, ROT_DIM // 2, 1) * s_dn)
    return jnp.concatenate(parts, axis=1)


def _inproj_kernel(x_ref, g_ref, wq_ref, wk_ref, wv_ref, wmq_ref, wmk_ref, wmv_ref, wmo_ref, wgt_ref,
                   bg_ref, cos_ref, sup_ref, sdn_ref,
                   q_ref, k_ref, v_ref, mq_ref, mk_ref, mv_ref, mo_ref, gt_ref, *kv_t_refs,
                   tiles_per_seq, first_kept):
    h = _rms(x_ref[...], g_ref[...]).astype(BF16)
    cos, s_up, s_dn = cos_ref[...], sup_ref[...], sdn_ref[...]

    def proj(w_ref):
        return jnp.dot(h, w_ref[...], preferred_element_type=F32)

    q_ref[...] = _rope(proj(wq_ref), cos, s_up, s_dn) * (ATTN_HEAD_DIM ** -0.5)
    k = _rope(proj(wk_ref), cos, s_up, s_dn)
    v = proj(wv_ref)
    k_ref[...] = k
    v_ref[...] = v
    if kv_t_refs:
        @pl.when(pl.program_id(0) % tiles_per_seq >= first_kept)
        def _():
            kv_t_refs[0][...] = k.T
            kv_t_refs[1][...] = v.T
    mq_ref[...] = proj(wmq_ref).astype(BF16)
    mk_ref[...] = (proj(wmk_ref) * (MLSTM_HEAD_DIM ** -0.5)).astype(BF16)
    mv_ref[...] = proj(wmv_ref).astype(BF16)
    mo_ref[...] = proj(wmo_ref)
    gt_ref[...] = proj(wgt_ref) + bg_ref[...]


def _inproj(x, g, ws, wgt, bg, tabs, tm, seq_len=None, keep=0):
    m, d = x.shape
    w = ws[0].shape[1]
    row = lambda width: pl.BlockSpec((tm, width), lambda i: (i, 0))
    nt = tabs[0].shape[0] // tm
    tab = pl.BlockSpec((tm, LANES), lambda i: (i % nt, 0))
    out_dt = (F32, F32, F32, BF16, BF16, BF16, F32)
    out_specs = [row(w)] * 7 + [row(LANES)]
    out_shape = ([jax.ShapeDtypeStruct((m, w), dt) for dt in out_dt]
                 + [jax.ShapeDtypeStruct((m, LANES), F32)])
    tiles_per_seq, first_kept = 1, 0
    if keep:
        assert seq_len % tm == 0 and keep % tm == 0
        tiles_per_seq, first_kept = seq_len // tm, (seq_len - keep) // tm
        kv_t = pl.BlockSpec((None, w, tm), lambda i: (i // tiles_per_seq, 0,
                                                      jnp.maximum(i % tiles_per_seq - first_kept, 0)))
        out_specs += [kv_t, kv_t]
        out_shape += [jax.ShapeDtypeStruct((m // seq_len, w, keep), F32)] * 2
    return pl.pallas_call(
        functools.partial(_inproj_kernel, tiles_per_seq=tiles_per_seq, first_kept=first_kept),
        grid=(m // tm,),
        in_specs=[row(d), _const_spec(g.shape)] + [_const_spec(wi.shape) for wi in ws]
                 + [_const_spec(wgt.shape), _const_spec(bg.shape), tab, tab, tab],
        out_specs=out_specs,
        out_shape=out_shape,
        compiler_params=_cparams(("arbitrary",)),
        name="inproj",
    )(x, g, *ws, wgt, bg, *tabs)


def _attn_kernel(q_ref, k_ref, v_ref, g_ref, o_ref, acc_ref, m_ref, l_ref, bias_ref):
    s_len = q_ref.shape[0]
    blk = STEPS
    lane = lax.broadcasted_iota(jnp.int32, (1, LANES), 1)
    head0 = lane < ATTN_HEAD_DIM
    qi = lax.broadcasted_iota(jnp.int32, (2 * blk, 2 * blk), 0) & (blk - 1)
    kj = lax.broadcasted_iota(jnp.int32, (2 * blk, 2 * blk), 1)
    band = (kj >= qi) & (kj <= qi + blk)
    bias_ref[0] = jnp.where(band & (kj >= blk), 0.0, -jnp.inf)
    bias_ref[1] = jnp.where(band, 0.0, -jnp.inf)

    def pick(a):
        return jnp.where(head0, a[:blk], a[blk:])

    def branch(dil, first, last):
        nblk = s_len // (dil * blk)

        rows = lambda start: pl.ds(start, blk, stride=dil) if dil > 1 else pl.ds(start, blk)

        def group(gi, carry):
            units = []
            for j in range(ATTN_GROUP):
                u = gi * ATTN_GROUP + j
                r = u // nblk
                n = u % nblk
                qstart = r + dil * blk * n
                pstart = r + dil * blk * jnp.maximum(n - 1, 0)
                if dil == 1:
                    qstart, pstart = pl.multiple_of(qstart, blk), pl.multiple_of(pstart, blk)
                units.append((n, qstart, pstart))
            old = [None if first else (m_ref[rows(qs_), :], l_ref[rows(qs_), :], acc_ref[rows(qs_), :])
                   for _, qs_, _ in units]
            scores, vals = [], []
            for n, qstart, pstart in units:
                q = q_ref[rows(qstart), :]
                kk = jnp.concatenate([k_ref[rows(pstart), :], k_ref[rows(qstart), :]], axis=0).astype(BF16)
                vals.append(jnp.concatenate([v_ref[rows(pstart), :], v_ref[rows(qstart), :]],
                                            axis=0).astype(BF16))
                qs = jnp.concatenate([jnp.where(head0, q, 0.0), jnp.where(head0, 0.0, q)],
                                     axis=0).astype(BF16)
                s = lax.dot_general(qs, kk, (((1,), (1,)), ((), ())), preferred_element_type=F32)
                scores.append(s + bias_ref[jnp.minimum(n, 1)])
            stats = []
            for s in scores:
                mb = jnp.max(s, axis=1, keepdims=True)
                p = jnp.exp(s - mb)
                stats.append((mb, jnp.sum(p, axis=1, keepdims=True), p.astype(BF16)))
            outs = [pick(jnp.dot(p, vv, preferred_element_type=F32)) for (_, _, p), vv in zip(stats, vals)]
            for (n, qstart, _), prev, (mb, lb, _), ob in zip(units, old, stats, outs):
                mb = pick(jnp.broadcast_to(mb, (2 * blk, LANES)))
                lb = pick(jnp.broadcast_to(lb, (2 * blk, LANES)))
                if first:
                    m_new, l_new, acc = mb, lb, ob
                else:
                    m_old, l_old, acc_old = prev
                    m_new = jnp.maximum(m_old, mb)
                    w_old = jnp.exp(m_old - m_new)
                    w_blk = jnp.exp(mb - m_new)
                    l_new = l_old * w_old + lb * w_blk
                    acc = acc_old * w_old + ob * w_blk
                if last:
                    y = acc / l_new
                    sq = y * y
                    ms0 = jnp.sum(jnp.where(head0, sq, 0.0), axis=1, keepdims=True)
                    ms1 = jnp.sum(jnp.where(head0, 0.0, sq), axis=1, keepdims=True)
                    ms = jnp.where(head0, ms0, ms1) * (1.0 / ATTN_HEAD_DIM)
                    o_ref[rows(qstart), :] = (y * lax.rsqrt(ms + EPS) * g_ref[...]).astype(o_ref.dtype)
                else:
                    m_ref[rows(qstart), :] = m_new
                    l_ref[rows(qstart), :] = l_new
                    acc_ref[rows(qstart), :] = acc
            return carry

        assert (dil * nblk) % ATTN_GROUP == 0
        lax.fori_loop(0, dil * nblk // ATTN_GROUP, group, 0)

    for idx, dil in enumerate(DILATIONS):
        branch(dil, idx == 0, idx == len(DILATIONS) - 1)


def _attention(q, k, v, g, batch):
    m, w = q.shape
    s_len = m // batch
    assert s_len % (max(DILATIONS) * STEPS) == 0
    nslab = w // LANES
    blk = pl.BlockSpec((s_len, LANES), lambda b, j: (b, j))
    return pl.pallas_call(
        _attn_kernel,
        grid=(batch, nslab),
        in_specs=[blk, blk, blk, pl.BlockSpec((1, LANES), lambda b, j: (0, j))],
        out_specs=blk,
        out_shape=jax.ShapeDtypeStruct((m, w), BF16),
        scratch_shapes=[pltpu.VMEM((s_len, LANES), F32)] * 3 + [pltpu.VMEM((2, 2 * STEPS, 2 * STEPS), F32)],
        compiler_params=_cparams(("parallel", "parallel")),
        name="attention",
    )(q, k, v, g)


def _log_sigmoid(x):
    return jnp.minimum(x, 0.0) - jnp.log1p(jnp.exp(-jnp.abs(x)))


def _head_out(h, mo, g):
    hm = jax.nn.sigmoid(mo) * h
    return hm * lax.rsqrt(jnp.mean(hm * hm, axis=-1, keepdims=True) + EPS) * g


def _mlstm_kernel(q_ref, k_ref, v_ref, mo_ref, gt_ref, g_ref, o_ref, c_out, n_out, m_out,
                  c_sc, n_sc, m_sc):
    chunk = pl.program_id(1)
    length = q_ref.shape[0]
    hd = MLSTM_HEAD_DIM

    @pl.when(chunk == 0)
    def _():
        c_sc[...] = jnp.zeros_like(c_sc)
        n_sc[...] = jnp.zeros_like(n_sc)
        m_sc[...] = jnp.zeros_like(m_sc)

    gates = gt_ref[...]
    lf = _log_sigmoid(gates)
    ti = lax.broadcasted_iota(jnp.int32, (length, length), 0)
    si = lax.broadcasted_iota(jnp.int32, (length, length), 1)
    causal = si <= ti
    tril = causal.astype(BF16)
    hi = lf.astype(BF16)
    rem = lf - hi.astype(F32)
    mid = rem.astype(BF16)
    lo = (rem - mid.astype(F32)).astype(BF16)
    bcum = (jnp.dot(tril, hi, preferred_element_type=F32) + jnp.dot(tril, mid, preferred_element_type=F32)
            + jnp.dot(tril, lo, preferred_element_type=F32))
    gates_t = gates.T
    bcum_t = bcum.T

    for h in range(MLSTM_HEADS):
        f = FORGET_COL + h
        sl = slice(hd * h, hd * (h + 1))
        q, k, v = q_ref[:, sl], k_ref[:, sl], v_ref[:, sl]
        b_col = bcum[:, f:f + 1]
        ig_col = gates[:, h:h + 1]
        b_row = bcum_t[f:f + 1, :]
        ig_row = gates_t[h:h + 1, :]
        m_prev = m_sc[h][0:1, 0:1]
        c_prev = c_sc[h]
        n_prev = n_sc[h][0:1, :]

        d = jnp.where(causal, b_col - b_row + ig_row, -jnp.inf)
        m_inter = b_col + m_prev
        m = jnp.maximum(m_inter, jnp.max(d, axis=1, keepdims=True))
        qk = lax.dot_general(q, k, (((1,), (1,)), ((), ())), preferred_element_type=F32)
        w_intra = jnp.exp(d - m) * qk
        w_inter = jnp.exp(m_inter - m)
        num = (w_inter * jnp.dot(q, c_prev.astype(BF16), preferred_element_type=F32)
               + jnp.dot(w_intra.astype(BF16), v, preferred_element_type=F32))
        den = (w_inter * jnp.sum(q.astype(F32) * n_prev, axis=1, keepdims=True)
               + jnp.sum(w_intra, axis=1, keepdims=True))
        hh = num / jnp.maximum(jnp.abs(den), jnp.exp(-m))
        o_ref[:, sl] = _head_out(hh, mo_ref[:, sl], g_ref[:, sl]).astype(o_ref.dtype)

        m_end = m[length - 1:length, :]
        b_end = b_col[length - 1:length, :]
        w_end = jnp.exp(b_end - b_col + ig_col - m_end)
        decay = jnp.exp(b_end + m_prev - m_end)
        kw = k.astype(F32) * w_end
        c_new = decay * c_prev + lax.dot_general(kw.astype(BF16), v, (((0,), (0,)), ((), ())),
                                                 preferred_element_type=F32)
        n_new = decay * n_prev + jnp.sum(kw, axis=0, keepdims=True)
        c_sc[h] = c_new
        n_sc[h] = jnp.broadcast_to(n_new, n_sc.shape[1:])
        m_sc[h] = jnp.broadcast_to(m_end, m_sc.shape[1:])

    @pl.when(chunk == pl.num_programs(1) - 1)
    def _():
        c_out[0] = c_sc[...]
        n_out[0] = n_sc[...][:, 0, :]
        m_out[0] = m_sc[...][:, 0, :]


def _mlstm(mq, mk, mv, mo, gates, g, batch, chunk_len):
    m, w = mq.shape
    s_len = m // batch
    nck = s_len // chunk_len
    nh, hd = MLSTM_HEADS, MLSTM_HEAD_DIM
    row = lambda width: pl.BlockSpec((chunk_len, width), lambda b, c: (b * nck + c, 0))
    return pl.pallas_call(
        _mlstm_kernel,
        grid=(batch, nck),
        in_specs=[row(w), row(w), row(w), row(w), row(LANES), pl.BlockSpec((1, w), lambda b, c: (0, 0))],
        out_specs=[row(w),
                   pl.BlockSpec((1, nh, hd, hd), lambda b, c: (b, 0, 0, 0)),
                   pl.BlockSpec((1, nh, hd), lambda b, c: (b, 0, 0)),
                   pl.BlockSpec((1, nh, LANES), lambda b, c: (b, 0, 0))],
        out_shape=[jax.ShapeDtypeStruct((m, w), BF16),
                   jax.ShapeDtypeStruct((batch, nh, hd, hd), F32),
                   jax.ShapeDtypeStruct((batch, nh, hd), F32),
                   jax.ShapeDtypeStruct((batch, nh, LANES), F32)],
        scratch_shapes=[pltpu.VMEM((nh, hd, hd), F32), pltpu.VMEM((nh, 8, hd), F32),
                        pltpu.VMEM((nh, 8, LANES), F32)],
        compiler_params=_cparams(("parallel", "arbitrary")),
        name="mlstm",
    )(mq, mk, mv, mo, gates, g)


def _out_ffn_kernel(x_ref, a_ref, mm_ref, woa_ref, wom_ref, g_ref, wg_ref, wu_ref, wd_ref, gf_ref,
                    o_ref, acc_ref):
    x = (x_ref[...] + jnp.dot(a_ref[...], woa_ref[...], preferred_element_type=F32)
         + jnp.dot(mm_ref[...], wom_ref[...], preferred_element_type=F32))
    x = _swiglu_ffn(x, g_ref, wg_ref, wu_ref, wd_ref, acc_ref)
    o_ref[...] = _rms(x, gf_ref[...])


def _out_ffn(x, a, mm, woa, wom, g, wg, wu, wd, gf, tm):
    m, d = x.shape
    row = lambda width: pl.BlockSpec((tm, width), lambda i: (i, 0))
    consts = (woa, wom, g, wg, wu, wd, gf)
    return pl.pallas_call(
        _out_ffn_kernel,
        grid=(m // tm,),
        in_specs=[row(d), row(a.shape[1]), row(mm.shape[1])] + [_const_spec(c.shape) for c in consts],
        out_specs=row(d),
        out_shape=jax.ShapeDtypeStruct((m, d), F32),
        scratch_shapes=[pltpu.VMEM((tm, d), F32)],
        compiler_params=_cparams(("parallel",)),
        name="out_ffn",
    )(x, a, mm, *consts)


def _sample_cache_kernel(q_ref, kn_ref, vn_ref, g_ref, ck_ref, cv_ref, o_ref, ok_ref, ov_ref):
    nh, hd, w = ck_ref.shape
    q, kn, vn = q_ref[...], kn_ref[...], vn_ref[...]
    head_row = lax.broadcasted_iota(jnp.int32, (nh, 1), 0)
    head_lane = lax.broadcasted_iota(jnp.int32, (1, nh), 1)
    dist = w - lax.broadcasted_iota(jnp.int32, (1, w), 1)

    s = jnp.zeros((nh, w), F32)
    s_self = jnp.zeros((nh, 1), F32)
    for h in range(nh):
        qc = q[:, h:h + 1]
        s = jnp.where(head_row == h, jnp.sum(ck_ref[h] * qc, axis=0, keepdims=True), s)
        s_self = jnp.where(head_row == h, jnp.sum(qc * kn[:, h:h + 1], axis=0, keepdims=True), s_self)

    ms, ps, pselfs, ls = [], [], [], []
    for dil in DILATIONS:
        in_branch = ((dist & (dil - 1)) == 0) & (dist <= STEPS * dil)
        sm = jnp.where(in_branch, s, -jnp.inf)
        mb = jnp.maximum(jnp.max(sm, axis=1, keepdims=True), s_self)
        p = jnp.exp(sm - mb)
        p_self = jnp.exp(s_self - mb)
        ms.append(mb)
        ps.append(p)
        pselfs.append(p_self)
        ls.append(jnp.sum(p, axis=1, keepdims=True) + p_self)
    m_max = functools.reduce(jnp.maximum, ms)
    ws = [jnp.exp(mb - m_max) for mb in ms]
    p_all = sum(wb * p for wb, p in zip(ws, ps))
    p_new = sum(wb * p for wb, p in zip(ws, pselfs))
    den = sum(wb * lb for wb, lb in zip(ws, ls))

    y = jnp.zeros((hd, nh), F32)
    for h in range(nh):
        o_h = (jnp.sum(cv_ref[h] * p_all[h:h + 1, :], axis=1, keepdims=True)
               + vn[:, h:h + 1] * p_new[h:h + 1, :])
        y = jnp.where(head_lane == h, o_h / den[h:h + 1, :], y)
    o_ref[...] = y * lax.rsqrt(jnp.mean(y * y, axis=0, keepdims=True) + EPS) * g_ref[...]

    last_lane = lax.broadcasted_iota(jnp.int32, (1, LANES), 1) == LANES - 1
    for c_ref, new, out_ref in ((ck_ref, kn, ok_ref), (cv_ref, vn, ov_ref)):
        for h in range(nh):
            shifted = pltpu.roll(c_ref[h], w - 1, 1)
            out_ref[h, :, :w - LANES] = shifted[:, :w - LANES]
            out_ref[h, :, w - LANES:] = jnp.where(last_lane, new[:, h:h + 1], shifted[:, w - LANES:])


def _sample_cache(q_t, kn_t, vn_t, g_t, cache_k_t, cache_v_t):
    b, nh, hd, w = cache_k_t.shape
    tok = pl.BlockSpec((None, hd, nh), lambda i: (i, 0, 0))
    blk = pl.BlockSpec((None, nh, hd, w), lambda i: (i, 0, 0, 0))
    return pl.pallas_call(
        _sample_cache_kernel,
        grid=(b,),
        in_specs=[tok, tok, tok, pl.BlockSpec((hd, nh), lambda i: (0, 0)), blk, blk],
        out_specs=[tok, blk, blk],
        out_shape=[jax.ShapeDtypeStruct((b, hd, nh), F32)] + [jax.ShapeDtypeStruct(cache_k_t.shape, F32)] * 2,
        compiler_params=_cparams(("parallel",)),
        name="sample_cache",
    )(q_t, kn_t, vn_t, g_t, cache_k_t, cache_v_t)


def _mlstm_step_kernel(q_ref, k_ref, v_ref, mo_ref, gt_ref, g_ref, c_ref, n_ref, m_ref,
                       o_ref, c_out, n_out, m_out):
    bb = q_ref.shape[0]
    hd = MLSTM_HEAD_DIM
    gates = gt_ref[...]
    lane = lax.broadcasted_iota(jnp.int32, (bb, LANES), 1)
    eye = (lax.broadcasted_iota(jnp.int32, (hd, hd), 0)
           == lax.broadcasted_iota(jnp.int32, (hd, hd), 1)).astype(BF16)
    m_all = jnp.zeros((bb, LANES), F32)
    for h in range(MLSTM_HEADS):
        sl = slice(hd * h, hd * (h + 1))
        q, k, v = q_ref[:, sl], k_ref[:, sl], v_ref[:, sl]
        qf, kf, vf = q.astype(F32), k.astype(F32), v.astype(F32)
        ig = gates[:, h:h + 1]
        lf = _log_sigmoid(gates[:, FORGET_COL + h:FORGET_COL + h + 1])
        m_prev = m_ref[:, h:h + 1]
        n_prev = n_ref[:, sl]
        m_inter = lf + m_prev
        m = jnp.maximum(m_inter, ig)
        w_in = jnp.exp(ig - m)
        w_inter = jnp.exp(m_inter - m)
        q_t = lax.dot_general(eye, q, (((1,), (1,)), ((), ())), preferred_element_type=F32)
        k_t = lax.dot_general(eye, k, (((1,), (1,)), ((), ())), preferred_element_type=F32)
        qc_rows = []
        for i in range(bb):
            c_prev = c_ref[0, i, h]
            c_bf = c_prev.astype(BF16).astype(F32)
            qc_rows.append(jnp.sum(q_t[:, i:i + 1] * c_bf, axis=0, keepdims=True))
            c_out[0, i, h] = (w_inter[i:i + 1, :] * c_prev
                              + (w_in[i:i + 1, :] * k_t[:, i:i + 1]) * vf[i:i + 1, :])
        qc = jnp.concatenate(qc_rows, axis=0)
        w_intra = w_in * jnp.sum(qf * kf, axis=1, keepdims=True)
        num = w_inter * qc + w_intra * vf
        den = w_inter * jnp.sum(qf * n_prev, axis=1, keepdims=True) + w_intra
        hh = num / jnp.maximum(jnp.abs(den), jnp.exp(-m))
        o_ref[:, sl] = _head_out(hh, mo_ref[:, sl], g_ref[:, sl]).astype(o_ref.dtype)
        n_out[:, sl] = w_inter * n_prev + w_in * kf
        m_all = jnp.where(lane == h, m, m_all)
    m_out[...] = m_all


def _mlstm_step(mq, mk, mv, mo, gates, g, state_c, state_n, state_m, bb):
    b, w = mq.shape
    nh, hd = MLSTM_HEADS, MLSTM_HEAD_DIM
    row = lambda width: pl.BlockSpec((bb, width), lambda i: (i, 0))
    c_spec = pl.BlockSpec((1, bb, nh, hd, hd), lambda i: (0, i, 0, 0, 0))
    return pl.pallas_call(
        _mlstm_step_kernel,
        grid=(b // bb,),
        in_specs=[row(w), row(w), row(w), row(w), row(LANES), pl.BlockSpec((1, w), lambda i: (0, 0)),
                  c_spec, row(w), row(nh)],
        out_specs=[row(w), c_spec, row(w), row(LANES)],
        out_shape=[jax.ShapeDtypeStruct((b, w), BF16),
                   jax.ShapeDtypeStruct(state_c.shape, F32),
                   jax.ShapeDtypeStruct((b, w), F32),
                   jax.ShapeDtypeStruct((b, LANES), F32)],
        compiler_params=_cparams(("parallel",)),
        name="mlstm_step",
    )(mq, mk, mv, mo, gates, g, state_c, state_n, state_m)


def _rope_tables(pos):
    half = ROT_DIM // 2
    inv = jnp.power(jnp.float32(ROPE_THETA), -jnp.arange(half, dtype=F32) * 2.0 / ROT_DIM)
    ang = pos.astype(F32)[:, None] * inv[None, :]
    cos, sin = jnp.cos(ang), jnp.sin(ang)
    rest = ATTN_HEAD_DIM - ROT_DIM
    one = jnp.ones((pos.shape[0], rest), F32)
    zero = jnp.zeros((pos.shape[0], rest), F32)
    zh = jnp.zeros_like(sin)
    head = lambda parts: jnp.tile(jnp.concatenate(parts, axis=1), (1, LANES // ATTN_HEAD_DIM))
    return head([cos, cos, one]), head([-sin, zh, zero]), head([zh, sin, zero])


def _ffn_weights(g, w_gate, w_up, w_down):
    d, f = w_gate.shape
    assert f % FF_CHUNK == 0
    return g.reshape(1, d), w_gate.astype(BF16), w_up.astype(BF16), w_down.astype(BF16)


def _row_tile(m, want):
    return want if m % want == 0 else m


def kernel(x_prompt, x_sample, cache_k, cache_v, state_C, state_n, state_m, norm_ffn1, w_ffn1_gate,
           w_ffn1_up, w_ffn1_down, norm_mix, w_in, b_gate, g_attn_out, g_mlstm_out, w_out, norm_ffn2,
           w_ffn2_gate, w_ffn2_up, w_ffn2_down, norm_final):
    batch, s_len, d = x_prompt.shape
    db, ds, _ = x_sample.shape
    assert norm_ffn1.shape[0] == 1 and ds == 1
    aw = g_attn_out.shape[1]
    mw = g_mlstm_out.shape[1]
    nh_attn = aw // ATTN_HEAD_DIM
    w_buf = cache_k.shape[2]
    assert w_buf == WINDOW_MAX and w_buf % (max(DILATIONS) * STEPS) == 0

    ffn1 = _ffn_weights(norm_ffn1[0], w_ffn1_gate[0], w_ffn1_up[0], w_ffn1_down[0])
    ffn2 = _ffn_weights(norm_ffn2[0], w_ffn2_gate[0], w_ffn2_up[0], w_ffn2_down[0])
    w_in_bf = w_in[0].astype(BF16)
    ws = [w_in_bf[:, i * aw:(i + 1) * aw] for i in range(7)]
    ngate = 2 * MLSTM_HEADS
    wgt = jnp.pad(w_in_bf[:, 7 * aw:], ((0, 0), (0, LANES - ngate)))
    bg = jnp.pad(b_gate[0], (0, LANES - ngate)).reshape(1, LANES)
    g_mix = norm_mix.reshape(1, d)
    g_attn = g_attn_out.reshape(1, aw)
    g_ml = g_mlstm_out.reshape(1, mw)
    wo = w_out[0].astype(BF16)
    woa, wom = wo[:aw], wo[aw:]
    g_fin = norm_final.reshape(1, d)

    def trunk_in(x, pos, tm, **window):
        x1 = _ffn(x, *ffn1, tm=tm)
        proj = _inproj(x1, g_mix, ws, wgt, bg, _rope_tables(pos), tm=tm, **window)
        return x1, proj

    def trunk_out(x1, a, mm, tm):
        return _out_ffn(x1, a, mm, woa, wom, *ffn2, g_fin, tm=tm)

    mp = batch * s_len
    tm_p = _row_tile(s_len, 512)
    keep = min(WINDOW_MAX, s_len)
    x1, (q, k, v, mq, mk, mv, mo, gates, k_t, v_t) = trunk_in(
        x_prompt.reshape(mp, d), jnp.arange(s_len), tm_p, seq_len=s_len, keep=keep)
    a = _attention(q, k, v, g_attn, batch)
    mm, c_p, n_p, m_p = _mlstm(mq, mk, mv, mo, gates, g_ml, batch, _row_tile(s_len, 256))
    y_prompt = trunk_out(x1, a, mm, tm_p).reshape(batch, s_len, d)
    window = lambda t: jnp.transpose(t.reshape(batch, nh_attn, ATTN_HEAD_DIM, keep), (0, 3, 1, 2))
    k_prompt, v_prompt = window(k_t), window(v_t)

    pos_s = jnp.full((db,), PAST_LEN, jnp.int32)
    x1s, (qs, ks, vs, mqs, mks, mvs, mos, gts) = trunk_in(x_sample.reshape(db, d), pos_s, db)
    tok_t = lambda t: jnp.transpose(t.reshape(db, nh_attn, ATTN_HEAD_DIM), (0, 2, 1))
    rows_minor = lambda c: jnp.transpose(c[0], (0, 2, 3, 1))
    a_t, k_t, v_t = _sample_cache(tok_t(qs), tok_t(ks), tok_t(vs), g_attn.reshape(nh_attn, ATTN_HEAD_DIM).T,
                                  rows_minor(cache_k), rows_minor(cache_v))
    a_s = jnp.transpose(a_t, (0, 2, 1)).reshape(db, aw).astype(BF16)
    k_s, v_s = jnp.transpose(k_t, (0, 3, 1, 2)), jnp.transpose(v_t, (0, 3, 1, 2))
    mm_s, c_s, n_s, m_s = _mlstm_step(mqs, mks, mvs, mos, gts, g_ml, state_C, state_n.reshape(db, mw),
                                      state_m[0], bb=16)
    n_s = n_s.reshape(state_n.shape)
    y_sample = trunk_out(x1s, a_s, mm_s, db).reshape(db, 1, d)

    return (y_prompt, y_sample,
            k_prompt[None], v_prompt[None], c_p[None], n_p[None], m_p[None, :, :, 0],
            k_s[None], v_s[None], c_s, n_s, m_s[None, :, :MLSTM_HEADS])
```

```python
import functools
import math

import jax
import jax.numpy as jnp
import numpy as np
from jax import lax
from jax.experimental import pallas as pl
from jax.experimental.pallas import tpu as pltpu

F32 = jnp.float32
BF16 = jnp.bfloat16

EPS = 1e-6
LOG2_E = math.log2(math.e)
ROPE_THETA = 500000.0
ATTN_HEAD_DIM = 64
ROT_DIM = ATTN_HEAD_DIM // 4
MLSTM_HEADS = 4
MLSTM_HEAD_DIM = 128
DILATIONS = (16, 4, 1)
STEPS = 128
WINDOW_MAX = 2048
PAST_LEN = 8192
FORGET_COL = MLSTM_HEADS

LANES = 128
FF_CHUNK = 256
ATTN_GROUP = 4
ATTN_CLASSES = 2
VMEM_LIMIT = 56 * 1024 * 1024


def _cparams(sem):
    return pltpu.CompilerParams(dimension_semantics=sem, vmem_limit_bytes=VMEM_LIMIT)


def _const_spec(shape):
    nd = len(shape)
    return pl.BlockSpec(shape, lambda *_: (0,) * nd, pipeline_mode=pl.Buffered(1))


def _rms(x, g):
    return x * lax.rsqrt(jnp.mean(x * x, axis=-1, keepdims=True) + EPS) * g


def _swiglu_ffn(x, g_ref, wg_ref, wu_ref, wd_ref, acc_ref):
    h = _rms(x, g_ref[...]).astype(BF16)
    for c in range(wg_ref.shape[1] // FF_CHUNK):
        cols = slice(c * FF_CHUNK, (c + 1) * FF_CHUNK)
        gt = jnp.dot(h, wg_ref[:, cols], preferred_element_type=F32)
        ut = jnp.dot(h, wu_ref[:, cols], preferred_element_type=F32)
        a = (gt * jax.nn.sigmoid(gt) * ut).astype(BF16)
        part = jnp.dot(a, wd_ref[cols, :], preferred_element_type=F32)
        if c == 0:
            acc_ref[...] = part
        else:
            acc_ref[...] += part
    return x + 0.5 * acc_ref[...]


def _ffn_kernel(x_ref, g_ref, wg_ref, wu_ref, wd_ref, o_ref, acc_ref):
    o_ref[...] = _swiglu_ffn(x_ref[...], g_ref, wg_ref, wu_ref, wd_ref, acc_ref)


def _ffn(x, g, wg, wu, wd, tm):
    m, d = x.shape
    row = pl.BlockSpec((tm, d), lambda i: (i, 0))
    return pl.pallas_call(
        _ffn_kernel,
        grid=(m // tm,),
        in_specs=[row, _const_spec(g.shape), _const_spec(wg.shape), _const_spec(wu.shape),
                  _const_spec(wd.shape)],
        out_specs=row,
        out_shape=jax.ShapeDtypeStruct((m, d), F32),
        scratch_shapes=[pltpu.VMEM((tm, d), F32)],
        compiler_params=_cparams(("parallel",)),
        name="ffn",
    )(x, g, wg, wu, wd)


def _rope(y, cos, s_up, s_dn):
    parts = []
    for i in range(y.shape[1] // LANES):
        ys = y[:, LANES * i:LANES * (i + 1)]
        parts.append(ys * cos + pltpu.roll(ys, LANES - ROT_DIM // 2, 1) * s_up
                     + pltpu.roll(ys, ROT_DIM // 2, 1) * s_dn)
    return jnp.concatenate(parts, axis=1)


def _inproj_kernel(x_ref, g_ref, wq_ref, wk_ref, wv_ref, wmq_ref, wmk_ref, wmv_ref, wmo_ref, wgt_ref,
                   bg_ref, cos_ref, sup_ref, sdn_ref,
                   q_ref, k_ref, v_ref, mq_ref, mk_ref, mv_ref, mo_ref, gt_ref, *kv_t_refs):
    h = _rms(x_ref[...], g_ref[...]).astype(BF16)
    cos, s_up, s_dn = cos_ref[...], sup_ref[...], sdn_ref[...]

    def proj(w_ref):
        return jnp.dot(h, w_ref[...], preferred_element_type=F32)

    q_ref[...] = _rope(proj(wq_ref), cos, s_up, s_dn) * (ATTN_HEAD_DIM ** -0.5 * LOG2_E)
    k = _rope(proj(wk_ref), cos, s_up, s_dn)
    v = proj(wv_ref)
    k_ref[...] = k
    v_ref[...] = v
    if kv_t_refs:
        kv_t_refs[0][...] = k.T
        kv_t_refs[1][...] = v.T
    mq_ref[...] = proj(wmq_ref).astype(BF16)
    mk_ref[...] = (proj(wmk_ref) * (MLSTM_HEAD_DIM ** -0.5)).astype(BF16)
    mv_ref[...] = proj(wmv_ref).astype(BF16)
    mo_ref[...] = proj(wmo_ref)
    gt_ref[...] = proj(wgt_ref) + bg_ref[...]


def _inproj(x, g, ws, wgt, bg, tabs, tm, seq_len=None, keep=0):
    m, d = x.shape
    w = ws[0].shape[1]
    row = lambda width: pl.BlockSpec((tm, width), lambda i: (i, 0))
    nt = tabs[0].shape[0] // tm
    tab = pl.BlockSpec((tm, LANES), lambda i: (i % nt, 0))
    out_dt = (F32, F32, F32, BF16, BF16, BF16, F32)
    out_specs = [row(w)] * 7 + [row(LANES)]
    out_shape = ([jax.ShapeDtypeStruct((m, w), dt) for dt in out_dt]
                 + [jax.ShapeDtypeStruct((m, LANES), F32)])
    if keep:
        assert seq_len % tm == 0 and keep % tm == 0
        tiles_per_seq, first_kept = seq_len // tm, (seq_len - keep) // tm
        kv_t = pl.BlockSpec((None, w, tm), lambda i: (i // tiles_per_seq, 0,
                                                      jnp.maximum(i % tiles_per_seq - first_kept, 0)))
        out_specs += [kv_t, kv_t]
        out_shape += [jax.ShapeDtypeStruct((m // seq_len, w, keep), F32)] * 2
    return pl.pallas_call(
        _inproj_kernel,
        grid=(m // tm,),
        in_specs=[row(d), _const_spec(g.shape)] + [_const_spec(wi.shape) for wi in ws]
                 + [_const_spec(wgt.shape), _const_spec(bg.shape), tab, tab, tab],
        out_specs=out_specs,
        out_shape=out_shape,
        compiler_params=_cparams(("arbitrary",)),
        name="inproj",
    )(x, g, *ws, wgt, bg, *tabs)


def _attn_kernel(q_ref, k_ref, v_ref, g_ref, o_ref, acc_ref, m_ref, l_ref, bias_ref):
    s_len = q_ref.shape[0]
    blk = STEPS
    lane = lax.broadcasted_iota(jnp.int32, (1, LANES), 1)
    head0 = lane < ATTN_HEAD_DIM
    qi = lax.broadcasted_iota(jnp.int32, (2 * blk, 2 * blk), 0) & (blk - 1)
    kj = lax.broadcasted_iota(jnp.int32, (2 * blk, 2 * blk), 1)
    band = (kj >= qi) & (kj <= qi + blk)
    bias_ref[0] = jnp.where(band & (kj >= blk), 0.0, -jnp.inf)
    bias_ref[1] = jnp.where(band, 0.0, -jnp.inf)

    ones_blk = jnp.ones((2 * blk, LANES), BF16)

    def pick(a):
        return jnp.where(head0, a[:blk], a[blk:])

    def branch(dil, first, last):
        nblk = s_len // (dil * blk)

        rows = lambda start: pl.ds(start, blk, stride=dil) if dil > 1 else pl.ds(start, blk)

        def run_units(units):
            old = [None if first else (m_ref[rows(qs_), :], l_ref[rows(qs_), :], acc_ref[rows(qs_), :])
                   for qs_, _, _ in units]
            loaded = {}

            def block(ref, start):
                key = (id(ref), id(start))
                if key not in loaded:
                    loaded[key] = ref[rows(start), :].astype(BF16)
                return loaded[key]

            cat = lambda parts: parts[0] if len(parts) == 1 else jnp.concatenate(parts, axis=0)
            scores, vals = [], []
            for qstart, kstarts, bias in units:
                q = q_ref[rows(qstart), :]
                kk = cat([block(k_ref, st) for st in kstarts])
                vv = cat([block(v_ref, st) for st in kstarts])
                vals.append(jnp.concatenate([vv, ones_blk[:vv.shape[0]]], axis=1))
                qs = jnp.concatenate([jnp.where(head0, q, 0.0), jnp.where(head0, 0.0, q)],
                                     axis=0).astype(BF16)
                s = lax.dot_general(qs, kk, (((1,), (1,)), ((), ())), preferred_element_type=F32)
                scores.append(s + bias)
            stats = []
            for s in scores:
                mb = jnp.max(s, axis=1, keepdims=True)
                p = jnp.exp2(s - mb)
                stats.append((mb, p.astype(BF16)))
            outs = [jnp.dot(p, vv, preferred_element_type=F32) for (_, p), vv in zip(stats, vals)]
            for (qstart, _, _), prev, (mb, _), res in zip(units, old, stats, outs):
                mb = pick(jnp.broadcast_to(mb, (2 * blk, LANES)))
                ob = pick(res[:, :LANES])
                lb = pick(res[:, LANES:])
                if first:
                    m_new, l_new, acc = mb, lb, ob
                else:
                    m_old, l_old, acc_old = prev
                    m_new = jnp.maximum(m_old, mb)
                    w_old = jnp.exp2(m_old - m_new)
                    w_blk = jnp.exp2(mb - m_new)
                    l_new = l_old * w_old + lb * w_blk
                    acc = acc_old * w_old + ob * w_blk
                if last:
                    y = acc / l_new
                    sq = y * y
                    ms0 = jnp.sum(jnp.where(head0, sq, 0.0), axis=1, keepdims=True)
                    ms1 = jnp.sum(jnp.where(head0, 0.0, sq), axis=1, keepdims=True)
                    ms = jnp.where(head0, ms0, ms1) * (1.0 / ATTN_HEAD_DIM)
                    o_ref[rows(qstart), :] = (y * lax.rsqrt(ms + EPS) * g_ref[...]).astype(o_ref.dtype)
                else:
                    m_ref[rows(qstart), :] = m_new
                    l_ref[rows(qstart), :] = l_new
                    acc_ref[rows(qstart), :] = acc

        def align(start):
            return pl.multiple_of(start, blk) if dil == 1 else start

        if nblk * ATTN_CLASSES <= ATTN_GROUP:
            def classes(ci, carry):
                units = []
                for j in range(ATTN_CLASSES):
                    starts = [align(ci * ATTN_CLASSES + j + dil * blk * n) for n in range(nblk)]
                    units.append((starts[0], starts[:1], bias_ref[1][:, blk:]))
                    units += [(starts[n], starts[n - 1:n + 1], bias_ref[1]) for n in range(1, nblk)]
                run_units(units)
                return carry

            assert dil % ATTN_CLASSES == 0
            lax.fori_loop(0, dil // ATTN_CLASSES, classes, 0)
        else:
            def group(gi, carry):
                units = []
                for j in range(ATTN_GROUP):
                    u = gi * ATTN_GROUP + j
                    r = u // nblk
                    n = u % nblk
                    units.append((align(r + dil * blk * n),
                                  [align(r + dil * blk * jnp.maximum(n - 1, 0)), align(r + dil * blk * n)],
                                  bias_ref[jnp.minimum(n, 1)]))
                run_units(units)
                return carry

            assert (dil * nblk) % ATTN_GROUP == 0
            lax.fori_loop(0, dil * nblk // ATTN_GROUP, group, 0)

    for idx, dil in enumerate(DILATIONS):
        branch(dil, idx == 0, idx == len(DILATIONS) - 1)


def _attention(q, k, v, g, batch):
    m, w = q.shape
    s_len = m // batch
    assert s_len % (max(DILATIONS) * STEPS) == 0
    nslab = w // LANES
    blk = pl.BlockSpec((s_len, LANES), lambda b, j: (b, j))
    return pl.pallas_call(
        _attn_kernel,
        grid=(batch, nslab),
        in_specs=[blk, blk, blk, pl.BlockSpec((1, LANES), lambda b, j: (0, j))],
        out_specs=blk,
        out_shape=jax.ShapeDtypeStruct((m, w), BF16),
        scratch_shapes=[pltpu.VMEM((s_len, LANES), F32)] * 3 + [pltpu.VMEM((2, 2 * STEPS, 2 * STEPS), F32)],
        compiler_params=_cparams(("parallel", "parallel")),
        name="attention",
    )(q, k, v, g)


def _log_sigmoid(x):
    return jnp.minimum(x, 0.0) - jnp.log1p(jnp.exp(-jnp.abs(x)))


def _head_out(h, mo, g):
    hm = jax.nn.sigmoid(mo) * h
    return hm * lax.rsqrt(jnp.mean(hm * hm, axis=-1, keepdims=True) + EPS) * g


def _cummax_rows(x):
    row = lax.broadcasted_iota(jnp.int32, x.shape, 0)
    k = 1
    while k < x.shape[0]:
        x = jnp.maximum(x, jnp.where(row >= k, pltpu.roll(x, k, 0), -jnp.inf))
        k *= 2
    return x


def _mlstm_kernel(q_ref, k_ref, v_ref, mo_ref, gt_ref, g_ref, o_ref, c_out, n_out, m_out, c_sc, m_sc):
    nseq, length = q_ref.shape[:2]
    chunk = pl.program_id(1)
    hd = MLSTM_HEAD_DIM

    @pl.when(chunk == 0)
    def _():
        c_sc[...] = jnp.zeros_like(c_sc)
        m_sc[...] = jnp.zeros_like(m_sc)

    ti = lax.broadcasted_iota(jnp.int32, (length, length), 0)
    si = lax.broadcasted_iota(jnp.int32, (length, length), 1)
    causal = si <= ti
    tril = causal.astype(BF16)
    one_col = (lax.broadcasted_iota(jnp.int32, (length, hd), 1) == 0).astype(BF16)

    gate_vals = []
    for s in range(nseq):
        gates = gt_ref[s]
        lf = _log_sigmoid(gates)
        hi = lf.astype(BF16)
        rem = lf - hi.astype(F32)
        mid = rem.astype(BF16)
        lo = (rem - mid.astype(F32)).astype(BF16)
        bcum = (jnp.dot(tril, hi, preferred_element_type=F32)
                + jnp.dot(tril, mid, preferred_element_type=F32)
                + jnp.dot(tril, lo, preferred_element_type=F32))
        b = pltpu.roll(bcum, LANES - FORGET_COL, 1)
        a = gates - b
        m_prev = m_sc[s][0:1, :]
        m_inter = b + m_prev
        m = jnp.maximum(m_inter, b + _cummax_rows(a))
        w_inter = jnp.exp(m_inter - m)
        u = b - m
        floor = jnp.exp(-m)
        a_t = a.T
        m_end = m[length - 1:length, :]
        b_end = b[length - 1:length, :]
        w_end = jnp.exp(a + (b_end - m_end))
        decay = jnp.exp(b_end + m_prev - m_end)
        m_sc[s] = jnp.broadcast_to(m_end, m_sc.shape[1:])
        gate_vals.append((w_inter, u, floor, a_t, w_end, decay))

    pairs = [(s, h) for s in range(nseq) for h in range(MLSTM_HEADS)]
    lanes = lambda h: slice(hd * h, hd * (h + 1))
    col = lambda h: slice(h, h + 1)
    nt_dims = (((1,), (1,)), ((), ()))
    tn_dims = (((0,), (0,)), ((), ()))
    qk_all = [lax.dot_general(q_ref[s, :, lanes(h)], k_ref[s, :, lanes(h)], nt_dims,
                              preferred_element_type=F32) for s, h in pairs]
    inter_all = [jnp.dot(q_ref[s, :, lanes(h)], c_sc[s, h].astype(BF16), preferred_element_type=F32)
                 for s, h in pairs]
    w_all = []
    for (s, h), qk in zip(pairs, qk_all):
        _, u, _, a_t, _, _ = gate_vals[s]
        w_all.append((jnp.exp(jnp.minimum(u[:, col(h)] + a_t[col(h), :], 0.0))
                      * jnp.where(causal, qk, 0.0)).astype(BF16))
    v_one = {(s, h): jnp.concatenate([v_ref[s, :, lanes(h)], one_col], axis=1) for s, h in pairs}
    intra_all = [jnp.dot(w, v_one[p], preferred_element_type=F32) for p, w in zip(pairs, w_all)]
    for (s, h), inter, intra in zip(pairs, inter_all, intra_all):
        w_inter, _, floor, _, _, _ = gate_vals[s]
        both = w_inter[:, col(h)] * inter + intra
        hh = both[:, :hd] / jnp.maximum(jnp.abs(both[:, hd:hd + 1]), floor[:, col(h)])
        o_ref[s, :, lanes(h)] = _head_out(hh, mo_ref[s, :, lanes(h)], g_ref[:, lanes(h)]).astype(o_ref.dtype)
    for s, h in pairs:
        _, _, _, _, w_end, decay = gate_vals[s]
        kw = (k_ref[s, :, lanes(h)].astype(F32) * w_end[:, col(h)]).astype(BF16)
        c_sc[s, h] = (decay[:, col(h)] * c_sc[s, h]
                      + lax.dot_general(kw, v_one[s, h], tn_dims, preferred_element_type=F32))

    @pl.when(chunk == pl.num_programs(1) - 1)
    def _():
        m_out[...] = m_sc[...]
        for s in range(nseq):
            for h in range(MLSTM_HEADS):
                c_fin = c_sc[s, h]
                c_out[s, h] = c_fin[:, :hd]
                n_out[s, h:h + 1, :] = c_fin[:, hd:].T[0:1, :]


def _mlstm(mq, mk, mv, mo, gates, g, batch, chunk_len):
    m, w = mq.shape
    s_len = m // batch
    nck = s_len // chunk_len
    nh, hd = MLSTM_HEADS, MLSTM_HEAD_DIM
    nseq = 2 if batch % 2 == 0 else 1
    groups = batch // nseq
    seqs = lambda t: t.reshape(groups, nseq, s_len, t.shape[-1])
    row = lambda width: pl.BlockSpec((None, nseq, chunk_len, width), lambda p, c: (p, 0, c, 0))
    state = lambda *dims: pl.BlockSpec((None, nseq) + dims, lambda p, c: (p, 0) + (0,) * len(dims))
    h_out, c_fin, n_fin, m_fin = pl.pallas_call(
        _mlstm_kernel,
        grid=(groups, nck),
        in_specs=[row(w), row(w), row(w), row(w), row(LANES), pl.BlockSpec((1, w), lambda p, c: (0, 0))],
        out_specs=[row(w), state(nh, hd, hd), state(nh, hd), state(8, LANES)],
        out_shape=[jax.ShapeDtypeStruct((groups, nseq, s_len, w), BF16),
                   jax.ShapeDtypeStruct((groups, nseq, nh, hd, hd), F32),
                   jax.ShapeDtypeStruct((groups, nseq, nh, hd), F32),
                   jax.ShapeDtypeStruct((groups, nseq, 8, LANES), F32)],
        scratch_shapes=[pltpu.VMEM((nseq, nh, hd, 2 * hd), F32), pltpu.VMEM((nseq, 8, LANES), F32)],
        compiler_params=_cparams(("parallel", "arbitrary")),
        name="mlstm",
    )(seqs(mq), seqs(mk), seqs(mv), seqs(mo), seqs(gates), g)
    return (h_out.reshape(m, w), c_fin.reshape(batch, nh, hd, hd), n_fin.reshape(batch, nh, hd),
            m_fin.reshape(batch, 8, LANES))


def _out_ffn_kernel(x_ref, a_ref, mm_ref, woa_ref, wom_ref, g_ref, wg_ref, wu_ref, wd_ref, gf_ref,
                    o_ref, acc_ref):
    x = (x_ref[...] + jnp.dot(a_ref[...], woa_ref[...], preferred_element_type=F32)
         + jnp.dot(mm_ref[...], wom_ref[...], preferred_element_type=F32))
    x = _swiglu_ffn(x, g_ref, wg_ref, wu_ref, wd_ref, acc_ref)
    o_ref[...] = _rms(x, gf_ref[...])


def _out_ffn(x, a, mm, woa, wom, g, wg, wu, wd, gf, tm):
    m, d = x.shape
    row = lambda width: pl.BlockSpec((tm, width), lambda i: (i, 0))
    consts = (woa, wom, g, wg, wu, wd, gf)
    return pl.pallas_call(
        _out_ffn_kernel,
        grid=(m // tm,),
        in_specs=[row(d), row(a.shape[1]), row(mm.shape[1])] + [_const_spec(c.shape) for c in consts],
        out_specs=row(d),
        out_shape=jax.ShapeDtypeStruct((m, d), F32),
        scratch_shapes=[pltpu.VMEM((tm, d), F32)],
        compiler_params=_cparams(("parallel",)),
        name="out_ffn",
    )(x, a, mm, *consts)


def _sample_cache_kernel(q_ref, kn_ref, vn_ref, g_ref, ck_ref, cv_ref, o_ref, ok_ref, ov_ref):
    nh, hd, w = ck_ref.shape
    q, kn, vn = q_ref[...], kn_ref[...], vn_ref[...]
    head_row = lax.broadcasted_iota(jnp.int32, (nh, 1), 0)
    head_lane = lax.broadcasted_iota(jnp.int32, (1, nh), 1)
    dist = w - lax.broadcasted_iota(jnp.int32, (1, w), 1)

    s = jnp.zeros((nh, w), F32)
    s_self = jnp.zeros((nh, 1), F32)
    for h in range(nh):
        qc = q[:, h:h + 1]
        s = jnp.where(head_row == h, jnp.sum(ck_ref[h] * qc, axis=0, keepdims=True), s)
        s_self = jnp.where(head_row == h, jnp.sum(qc * kn[:, h:h + 1], axis=0, keepdims=True), s_self)

    ms, ps, pselfs, ls = [], [], [], []
    for dil in DILATIONS:
        in_branch = ((dist & (dil - 1)) == 0) & (dist <= STEPS * dil)
        sm = jnp.where(in_branch, s, -jnp.inf)
        mb = jnp.maximum(jnp.max(sm, axis=1, keepdims=True), s_self)
        p = jnp.exp2(sm - mb)
        p_self = jnp.exp2(s_self - mb)
        ms.append(mb)
        ps.append(p)
        pselfs.append(p_self)
        ls.append(jnp.sum(p, axis=1, keepdims=True) + p_self)
    m_max = functools.reduce(jnp.maximum, ms)
    ws = [jnp.exp2(mb - m_max) for mb in ms]
    p_all = sum(wb * p for wb, p in zip(ws, ps))
    p_new = sum(wb * p for wb, p in zip(ws, pselfs))
    den = sum(wb * lb for wb, lb in zip(ws, ls))

    y = jnp.zeros((hd, nh), F32)
    for h in range(nh):
        o_h = (jnp.sum(cv_ref[h] * p_all[h:h + 1, :], axis=1, keepdims=True)
               + vn[:, h:h + 1] * p_new[h:h + 1, :])
        y = jnp.where(head_lane == h, o_h / den[h:h + 1, :], y)
    o_ref[...] = y * lax.rsqrt(jnp.mean(y * y, axis=0, keepdims=True) + EPS) * g_ref[...]

    last_lane = lax.broadcasted_iota(jnp.int32, (1, LANES), 1) == LANES - 1
    for c_ref, new, out_ref in ((ck_ref, kn, ok_ref), (cv_ref, vn, ov_ref)):
        for h in range(nh):
            shifted = pltpu.roll(c_ref[h], w - 1, 1)
            out_ref[h, :, :w - LANES] = shifted[:, :w - LANES]
            out_ref[h, :, w - LANES:] = jnp.where(last_lane, new[:, h:h + 1], shifted[:, w - LANES:])


def _sample_cache(q_t, kn_t, vn_t, g_t, cache_k_t, cache_v_t):
    b, nh, hd, w = cache_k_t.shape
    tok = pl.BlockSpec((None, hd, nh), lambda i: (i, 0, 0))
    blk = pl.BlockSpec((None, nh, hd, w), lambda i: (i, 0, 0, 0))
    return pl.pallas_call(
        _sample_cache_kernel,
        grid=(b,),
        in_specs=[tok, tok, tok, pl.BlockSpec((hd, nh), lambda i: (0, 0)), blk, blk],
        out_specs=[tok, blk, blk],
        out_shape=[jax.ShapeDtypeStruct((b, hd, nh), F32)] + [jax.ShapeDtypeStruct(cache_k_t.shape, F32)] * 2,
        compiler_params=_cparams(("parallel",)),
        name="sample_cache",
    )(q_t, kn_t, vn_t, g_t, cache_k_t, cache_v_t)


def _mlstm_step_kernel(q_ref, k_ref, v_ref, mo_ref, gt_ref, g_ref, c_ref, n_ref, m_ref,
                       o_ref, c_out, n_out, m_out):
    bb = q_ref.shape[0]
    hd = MLSTM_HEAD_DIM
    gates = gt_ref[...]
    lane = lax.broadcasted_iota(jnp.int32, (bb, LANES), 1)
    eye = (lax.broadcasted_iota(jnp.int32, (hd, hd), 0)
           == lax.broadcasted_iota(jnp.int32, (hd, hd), 1)).astype(BF16)
    m_all = jnp.zeros((bb, LANES), F32)
    for h in range(MLSTM_HEADS):
        sl = slice(hd * h, hd * (h + 1))
        q, k, v = q_ref[:, sl], k_ref[:, sl], v_ref[:, sl]
        qf, kf, vf = q.astype(F32), k.astype(F32), v.astype(F32)
        ig = gates[:, h:h + 1]
        lf = _log_sigmoid(gates[:, FORGET_COL + h:FORGET_COL + h + 1])
        m_prev = m_ref[:, h:h + 1]
        n_prev = n_ref[:, sl]
        m_inter = lf + m_prev
        m = jnp.maximum(m_inter, ig)
        w_in = jnp.exp(ig - m)
        w_inter = jnp.exp(m_inter - m)
        q_t = lax.dot_general(eye, q, (((1,), (1,)), ((), ())), preferred_element_type=F32)
        k_t = lax.dot_general(eye, k, (((1,), (1,)), ((), ())), preferred_element_type=F32)
        qc_rows = []
        for i in range(bb):
            c_prev = c_ref[0, i, h]
            c_bf = c_prev.astype(BF16).astype(F32)
            qc_rows.append(jnp.sum(q_t[:, i:i + 1] * c_bf, axis=0, keepdims=True))
            c_out[0, i, h] = (w_inter[i:i + 1, :] * c_prev
                              + (w_in[i:i + 1, :] * k_t[:, i:i + 1]) * vf[i:i + 1, :])
        qc = jnp.concatenate(qc_rows, axis=0)
        w_intra = w_in * jnp.sum(qf * kf, axis=1, keepdims=True)
        num = w_inter * qc + w_intra * vf
        den = w_inter * jnp.sum(qf * n_prev, axis=1, keepdims=True) + w_intra
        hh = num / jnp.maximum(jnp.abs(den), jnp.exp(-m))
        o_ref[:, sl] = _head_out(hh, mo_ref[:, sl], g_ref[:, sl]).astype(o_ref.dtype)
        n_out[:, sl] = w_inter * n_prev + w_in * kf
        m_all = jnp.where(lane == h, m, m_all)
    m_out[...] = m_all


def _mlstm_step(mq, mk, mv, mo, gates, g, state_c, state_n, state_m, bb):
    b, w = mq.shape
    nh, hd = MLSTM_HEADS, MLSTM_HEAD_DIM
    row = lambda width: pl.BlockSpec((bb, width), lambda i: (i, 0))
    c_spec = pl.BlockSpec((1, bb, nh, hd, hd), lambda i: (0, i, 0, 0, 0))
    return pl.pallas_call(
        _mlstm_step_kernel,
        grid=(b // bb,),
        in_specs=[row(w), row(w), row(w), row(w), row(LANES), pl.BlockSpec((1, w), lambda i: (0, 0)),
                  c_spec, row(w), row(nh)],
        out_specs=[row(w), c_spec, row(w), row(LANES)],
        out_shape=[jax.ShapeDtypeStruct((b, w), BF16),
                   jax.ShapeDtypeStruct(state_c.shape, F32),
                   jax.ShapeDtypeStruct((b, w), F32),
                   jax.ShapeDtypeStruct((b, LANES), F32)],
        compiler_params=_cparams(("parallel",)),
        name="mlstm_step",
    )(mq, mk, mv, mo, gates, g, state_c, state_n, state_m)


def _rope_tables(pos):
    half = ROT_DIM // 2
    inv = jnp.power(jnp.float32(ROPE_THETA), -jnp.arange(half, dtype=F32) * 2.0 / ROT_DIM)
    ang = pos.astype(F32)[:, None] * inv[None, :]
    cos, sin = jnp.cos(ang), jnp.sin(ang)
    rest = ATTN_HEAD_DIM - ROT_DIM
    one = jnp.ones((pos.shape[0], rest), F32)
    zero = jnp.zeros((pos.shape[0], rest), F32)
    zh = jnp.zeros_like(sin)
    head = lambda parts: jnp.tile(jnp.concatenate(parts, axis=1), (1, LANES // ATTN_HEAD_DIM))
    return head([cos, cos, one]), head([-sin, zh, zero]), head([zh, sin, zero])


def _ffn_weights(g, w_gate, w_up, w_down):
    d, f = w_gate.shape
    assert f % FF_CHUNK == 0
    return g.reshape(1, d), w_gate.astype(BF16), w_up.astype(BF16), w_down.astype(BF16)


def _row_tile(m, want):
    return want if m % want == 0 else m


def kernel(x_prompt, x_sample, cache_k, cache_v, state_C, state_n, state_m, norm_ffn1, w_ffn1_gate,
           w_ffn1_up, w_ffn1_down, norm_mix, w_in, b_gate, g_attn_out, g_mlstm_out, w_out, norm_ffn2,
           w_ffn2_gate, w_ffn2_up, w_ffn2_down, norm_final):
    batch, s_len, d = x_prompt.shape
    db, ds, _ = x_sample.shape
    assert norm_ffn1.shape[0] == 1 and ds == 1
    aw = g_attn_out.shape[1]
    mw = g_mlstm_out.shape[1]
    nh_attn = aw // ATTN_HEAD_DIM
    w_buf = cache_k.shape[2]
    assert w_buf == WINDOW_MAX and w_buf % (max(DILATIONS) * STEPS) == 0

    ffn1 = _ffn_weights(norm_ffn1[0], w_ffn1_gate[0], w_ffn1_up[0], w_ffn1_down[0])
    ffn2 = _ffn_weights(norm_ffn2[0], w_ffn2_gate[0], w_ffn2_up[0], w_ffn2_down[0])
    w_in_bf = w_in[0].astype(BF16)
    ws = [w_in_bf[:, i * aw:(i + 1) * aw] for i in range(7)]
    ngate = 2 * MLSTM_HEADS
    wgt = jnp.pad(w_in_bf[:, 7 * aw:], ((0, 0), (0, LANES - ngate)))
    bg = jnp.pad(b_gate[0], (0, LANES - ngate)).reshape(1, LANES)
    g_mix = norm_mix.reshape(1, d)
    g_attn = g_attn_out.reshape(1, aw)
    g_ml = g_mlstm_out.reshape(1, mw)
    wo = w_out[0].astype(BF16)
    woa, wom = wo[:aw], wo[aw:]
    g_fin = norm_final.reshape(1, d)

    def trunk_in(x, pos, tm, **window):
        x1 = _ffn(x, *ffn1, tm=tm)
        proj = _inproj(x1, g_mix, ws, wgt, bg, _rope_tables(pos), tm=tm, **window)
        return x1, proj

    def trunk_out(x1, a, mm, tm):
        return _out_ffn(x1, a, mm, woa, wom, *ffn2, g_fin, tm=tm)

    mp = batch * s_len
    tm_p = _row_tile(s_len, 512)
    keep = min(WINDOW_MAX, s_len)
    x1, (q, k, v, mq, mk, mv, mo, gates, k_t, v_t) = trunk_in(
        x_prompt.reshape(mp, d), jnp.arange(s_len), tm_p, seq_len=s_len, keep=keep)
    a = _attention(q, k, v, g_attn, batch)
    mm, c_p, n_p, m_p = _mlstm(mq, mk, mv, mo, gates, g_ml, batch, _row_tile(s_len, 256))
    y_prompt = trunk_out(x1, a, mm, tm_p).reshape(batch, s_len, d)
    window = lambda t: jnp.transpose(t.reshape(batch, nh_attn, ATTN_HEAD_DIM, keep), (0, 3, 1, 2))
    k_prompt, v_prompt = window(k_t), window(v_t)

    pos_s = jnp.full((db,), PAST_LEN, jnp.int32)
    x1s, (qs, ks, vs, mqs, mks, mvs, mos, gts) = trunk_in(x_sample.reshape(db, d), pos_s, db)
    tok_t = lambda t: jnp.transpose(t.reshape(db, nh_attn, ATTN_HEAD_DIM), (0, 2, 1))
    rows_minor = lambda c: jnp.transpose(c[0], (0, 2, 3, 1))
    a_t, k_t, v_t = _sample_cache(tok_t(qs), tok_t(ks), tok_t(vs), g_attn.reshape(nh_attn, ATTN_HEAD_DIM).T,
                                  rows_minor(cache_k), rows_minor(cache_v))
    a_s = jnp.transpose(a_t, (0, 2, 1)).reshape(db, aw).astype(BF16)
    k_s, v_s = jnp.transpose(k_t, (0, 3, 1, 2)), jnp.transpose(v_t, (0, 3, 1, 2))
    mm_s, c_s, n_s, m_s = _mlstm_step(mqs, mks, mvs, mos, gts, g_ml, state_C, state_n.reshape(db, mw),
                                      state_m[0], bb=16)
    n_s = n_s.reshape(state_n.shape)
    y_sample = trunk_out(x1s, a_s, mm_s, db).reshape(db, 1, d)

    return (y_prompt, y_sample,
            k_prompt[None], v_prompt[None], c_p[None], n_p[None], m_p[None, :, 0, :MLSTM_HEADS],
            k_s[None], v_s[None], c_s, n_s, m_s[None, :, :MLSTM_HEADS])
```

```python
import functools
import math

import jax
import jax.numpy as jnp
import numpy as np
from jax import lax
from jax.experimental import pallas as pl
from jax.experimental.pallas import tpu as pltpu

F32 = jnp.float32
BF16 = jnp.bfloat16

EPS = 1e-6
LOG2_E = math.log2(math.e)
ROPE_THETA = 500000.0
ATTN_HEAD_DIM = 64
ROT_DIM = ATTN_HEAD_DIM // 4
MLSTM_HEADS = 4
MLSTM_HEAD_DIM = 128
DILATIONS = (16, 4, 1)
STEPS = 128
WINDOW_MAX = 2048
PAST_LEN = 8192
FORGET_COL = MLSTM_HEADS

LANES = 128
FF_CHUNK = 256
ATTN_GROUP = 4
ATTN_CLASSES = 2
VMEM_LIMIT = 56 * 1024 * 1024


def _cparams(sem):
    return pltpu.CompilerParams(dimension_semantics=sem, vmem_limit_bytes=VMEM_LIMIT)


def _const_spec(shape):
    nd = len(shape)
    return pl.BlockSpec(shape, lambda *_: (0,) * nd, pipeline_mode=pl.Buffered(1))


def _rms(x, g):
    return x * lax.rsqrt(jnp.mean(x * x, axis=-1, keepdims=True) + EPS) * g


def _swiglu_ffn(x, g_ref, wg_ref, wu_ref, wd_ref, acc_ref):
    h = _rms(x, g_ref[...]).astype(BF16)
    for c in range(wg_ref.shape[1] // FF_CHUNK):
        cols = slice(c * FF_CHUNK, (c + 1) * FF_CHUNK)
        gt = jnp.dot(h, wg_ref[:, cols], preferred_element_type=F32)
        ut = jnp.dot(h, wu_ref[:, cols], preferred_element_type=F32)
        a = (gt * jax.nn.sigmoid(gt) * ut).astype(BF16)
        part = jnp.dot(a, wd_ref[cols, :], preferred_element_type=F32)
        if c == 0:
            acc_ref[...] = part
        else:
            acc_ref[...] += part
    return x + 0.5 * acc_ref[...]


def _ffn_kernel(x_ref, g_ref, wg_ref, wu_ref, wd_ref, *rest):
    cache_in = rest[:CACHE_IN] if len(rest) > 2 else ()
    o_ref, cache_out, acc_ref = rest[len(cache_in)], rest[len(cache_in) + 1:-1], rest[-1]
    o_ref[...] = _swiglu_ffn(x_ref[...], g_ref, wg_ref, wu_ref, wd_ref, acc_ref)
    if cache_in:
        _sample_cache_kernel(*cache_in, *cache_out)


def _ffn(x, g, wg, wu, wd, tm, cache=None):
    m, d = x.shape
    row = pl.BlockSpec((tm, d), lambda i: (i, 0))
    c_ops, c_in, c_out, c_shape = _hosted_cache(cache, 0, CACHE_PARTS) if cache else ([], [], [], [])
    res = pl.pallas_call(
        _ffn_kernel,
        grid=(m // tm,),
        in_specs=[row, _const_spec(g.shape), _const_spec(wg.shape), _const_spec(wu.shape),
                  _const_spec(wd.shape)] + c_in,
        out_specs=[row] + c_out,
        out_shape=[jax.ShapeDtypeStruct((m, d), F32)] + c_shape,
        scratch_shapes=[pltpu.VMEM((tm, d), F32)],
        compiler_params=_cparams(("parallel",)),
        name="ffn",
    )(x, g, wg, wu, wd, *c_ops)
    return res if cache else res[0]


def _rope(y, cos, s_up, s_dn):
    parts = []
    for i in range(y.shape[1] // LANES):
        ys = y[:, LANES * i:LANES * (i + 1)]
        parts.append(ys * cos + pltpu.roll(ys, LANES - ROT_DIM // 2, 1) * s_up
                     + pltpu.roll(ys, ROT_DIM // 2, 1) * s_dn)
    return jnp.concatenate(parts, axis=1)


def _inproj_kernel(x_ref, g_ref, wq_ref, wk_ref, wv_ref, wmq_ref, wmk_ref, wmv_ref, wmo_ref, wgt_ref,
                   bg_ref, cos_ref, sup_ref, sdn_ref,
                   q_ref, k_ref, v_ref, mq_ref, mk_ref, mv_ref, mo_ref, gt_ref, *kv_t_refs):
    h = _rms(x_ref[...], g_ref[...]).astype(BF16)
    cos, s_up, s_dn = cos_ref[...], sup_ref[...], sdn_ref[...]

    def proj(w_ref):
        return jnp.dot(h, w_ref[...], preferred_element_type=F32)

    q_ref[...] = _rope(proj(wq_ref), cos, s_up, s_dn) * (ATTN_HEAD_DIM ** -0.5 * LOG2_E)
    k = _rope(proj(wk_ref), cos, s_up, s_dn)
    v = proj(wv_ref)
    k_ref[...] = k
    v_ref[...] = v
    if kv_t_refs:
        kv_t_refs[0][...] = k.T
        kv_t_refs[1][...] = v.T
    mq_ref[...] = proj(wmq_ref).astype(BF16)
    mk_ref[...] = (proj(wmk_ref) * (MLSTM_HEAD_DIM ** -0.5)).astype(BF16)
    mv_ref[...] = proj(wmv_ref).astype(BF16)
    mo_ref[...] = proj(wmo_ref)
    gt_ref[...] = proj(wgt_ref) + bg_ref[...]


def _inproj(x, g, ws, wgt, bg, tabs, tm, seq_len=None, keep=0):
    m, d = x.shape
    w = ws[0].shape[1]
    row = lambda width: pl.BlockSpec((tm, width), lambda i: (i, 0))
    nt = tabs[0].shape[0] // tm
    tab = pl.BlockSpec((tm, LANES), lambda i: (i % nt, 0))
    out_dt = (F32, F32, F32, BF16, BF16, BF16, F32)
    out_specs = [row(w)] * 7 + [row(LANES)]
    out_shape = ([jax.ShapeDtypeStruct((m, w), dt) for dt in out_dt]
                 + [jax.ShapeDtypeStruct((m, LANES), F32)])
    if keep:
        assert seq_len % tm == 0 and keep % tm == 0
        tiles_per_seq, first_kept = seq_len // tm, (seq_len - keep) // tm
        kv_t = pl.BlockSpec((None, w, tm), lambda i: (i // tiles_per_seq, 0,
                                                      jnp.maximum(i % tiles_per_seq - first_kept, 0)))
        out_specs += [kv_t, kv_t]
        out_shape += [jax.ShapeDtypeStruct((m // seq_len, w, keep), F32)] * 2
    return pl.pallas_call(
        _inproj_kernel,
        grid=(m // tm,),
        in_specs=[row(d), _const_spec(g.shape)] + [_const_spec(wi.shape) for wi in ws]
                 + [_const_spec(wgt.shape), _const_spec(bg.shape), tab, tab, tab],
        out_specs=out_specs,
        out_shape=out_shape,
        compiler_params=_cparams(("arbitrary",)),
        name="inproj",
    )(x, g, *ws, wgt, bg, *tabs)


def _attn_kernel(q_ref, k_ref, v_ref, g_ref, o_ref, acc_ref, m_ref, l_ref, bias_ref):
    s_len = q_ref.shape[0]
    blk = STEPS
    lane = lax.broadcasted_iota(jnp.int32, (1, LANES), 1)
    head0 = lane < ATTN_HEAD_DIM
    qi = lax.broadcasted_iota(jnp.int32, (2 * blk, 2 * blk), 0) & (blk - 1)
    kj = lax.broadcasted_iota(jnp.int32, (2 * blk, 2 * blk), 1)
    band = (kj >= qi) & (kj <= qi + blk)
    bias_ref[0] = jnp.where(band & (kj >= blk), 0.0, -jnp.inf)
    bias_ref[1] = jnp.where(band, 0.0, -jnp.inf)

    ones_blk = jnp.ones((2 * blk, LANES), BF16)

    def pick(a):
        return jnp.where(head0, a[:blk], a[blk:])

    def branch(dil, first, last):
        nblk = s_len // (dil * blk)

        rows = lambda start: pl.ds(start, blk, stride=dil) if dil > 1 else pl.ds(start, blk)

        def run_units(units):
            old = [None if first else (m_ref[rows(qs_), :], l_ref[rows(qs_), :], acc_ref[rows(qs_), :])
                   for qs_, _, _ in units]
            loaded = {}

            def block(ref, start):
                key = (id(ref), id(start))
                if key not in loaded:
                    loaded[key] = ref[rows(start), :].astype(BF16)
                return loaded[key]

            cat = lambda parts: parts[0] if len(parts) == 1 else jnp.concatenate(parts, axis=0)
            scores, vals = [], []
            for qstart, kstarts, bias in units:
                q = q_ref[rows(qstart), :]
                kk = cat([block(k_ref, st) for st in kstarts])
                vv = cat([block(v_ref, st) for st in kstarts])
                vals.append(jnp.concatenate([vv, ones_blk[:vv.shape[0]]], axis=1))
                qs = jnp.concatenate([jnp.where(head0, q, 0.0), jnp.where(head0, 0.0, q)],
                                     axis=0).astype(BF16)
                s = lax.dot_general(qs, kk, (((1,), (1,)), ((), ())), preferred_element_type=F32)
                scores.append(s + bias)
            stats = []
            for s in scores:
                mb = jnp.max(s, axis=1, keepdims=True)
                p = jnp.exp2(s - mb)
                stats.append((mb, p.astype(BF16)))
            outs = [jnp.dot(p, vv, preferred_element_type=F32) for (_, p), vv in zip(stats, vals)]
            for (qstart, _, _), prev, (mb, _), res in zip(units, old, stats, outs):
                mb = pick(jnp.broadcast_to(mb, (2 * blk, LANES)))
                ob = pick(res[:, :LANES])
                lb = pick(res[:, LANES:])
                if first:
                    m_new, l_new, acc = mb, lb, ob
                else:
                    m_old, l_old, acc_old = prev
                    m_new = jnp.maximum(m_old, mb)
                    w_old = jnp.exp2(m_old - m_new)
                    w_blk = jnp.exp2(mb - m_new)
                    l_new = l_old * w_old + lb * w_blk
                    acc = acc_old * w_old + ob * w_blk
                if last:
                    y = acc / l_new
                    sq = y * y
                    ms0 = jnp.sum(jnp.where(head0, sq, 0.0), axis=1, keepdims=True)
                    ms1 = jnp.sum(jnp.where(head0, 0.0, sq), axis=1, keepdims=True)
                    ms = jnp.where(head0, ms0, ms1) * (1.0 / ATTN_HEAD_DIM)
                    o_ref[rows(qstart), :] = (y * lax.rsqrt(ms + EPS) * g_ref[...]).astype(o_ref.dtype)
                else:
                    m_ref[rows(qstart), :] = m_new
                    l_ref[rows(qstart), :] = l_new
                    acc_ref[rows(qstart), :] = acc

        def align(start):
            return pl.multiple_of(start, blk) if dil == 1 else start

        if nblk * ATTN_CLASSES <= ATTN_GROUP:
            def classes(ci, carry):
                units = []
                for j in range(ATTN_CLASSES):
                    starts = [align(ci * ATTN_CLASSES + j + dil * blk * n) for n in range(nblk)]
                    units.append((starts[0], starts[:1], bias_ref[1][:, blk:]))
                    units += [(starts[n], starts[n - 1:n + 1], bias_ref[1]) for n in range(1, nblk)]
                run_units(units)
                return carry

            assert dil % ATTN_CLASSES == 0
            lax.fori_loop(0, dil // ATTN_CLASSES, classes, 0)
        else:
            def group(gi, carry):
                units = []
                for j in range(ATTN_GROUP):
                    u = gi * ATTN_GROUP + j
                    r = u // nblk
                    n = u % nblk
                    units.append((align(r + dil * blk * n),
                                  [align(r + dil * blk * jnp.maximum(n - 1, 0)), align(r + dil * blk * n)],
                                  bias_ref[jnp.minimum(n, 1)]))
                run_units(units)
                return carry

            assert (dil * nblk) % ATTN_GROUP == 0
            lax.fori_loop(0, dil * nblk // ATTN_GROUP, group, 0)

    for idx, dil in enumerate(DILATIONS):
        branch(dil, idx == 0, idx == len(DILATIONS) - 1)


def _attention(q, k, v, g, batch):
    m, w = q.shape
    s_len = m // batch
    assert s_len % (max(DILATIONS) * STEPS) == 0
    nslab = w // LANES
    blk = pl.BlockSpec((s_len, LANES), lambda b, j: (b, j))
    return pl.pallas_call(
        _attn_kernel,
        grid=(batch, nslab),
        in_specs=[blk, blk, blk, pl.BlockSpec((1, LANES), lambda b, j: (0, j))],
        out_specs=blk,
        out_shape=jax.ShapeDtypeStruct((m, w), BF16),
        scratch_shapes=[pltpu.VMEM((s_len, LANES), F32)] * 3 + [pltpu.VMEM((2, 2 * STEPS, 2 * STEPS), F32)],
        compiler_params=_cparams(("parallel", "parallel")),
        name="attention",
    )(q, k, v, g)


def _log_sigmoid(x):
    return jnp.minimum(x, 0.0) - jnp.log1p(jnp.exp(-jnp.abs(x)))


def _head_out(h, mo, g):
    hm = jax.nn.sigmoid(mo) * h
    return hm * lax.rsqrt(jnp.mean(hm * hm, axis=-1, keepdims=True) + EPS) * g


def _cummax_rows(x):
    row = lax.broadcasted_iota(jnp.int32, x.shape, 0)
    k = 1
    while k < x.shape[0]:
        x = jnp.maximum(x, jnp.where(row >= k, pltpu.roll(x, k, 0), -jnp.inf))
        k *= 2
    return x


def _mlstm_kernel(q_ref, k_ref, v_ref, mo_ref, gt_ref, g_ref, o_ref, c_out, n_out, m_out, c_sc, m_sc):
    nseq, length = q_ref.shape[:2]
    chunk = pl.program_id(1)
    hd = MLSTM_HEAD_DIM

    @pl.when(chunk == 0)
    def _():
        c_sc[...] = jnp.zeros_like(c_sc)
        m_sc[...] = jnp.zeros_like(m_sc)

    ti = lax.broadcasted_iota(jnp.int32, (length, length), 0)
    si = lax.broadcasted_iota(jnp.int32, (length, length), 1)
    causal = si <= ti
    tril = causal.astype(BF16)
    one_col = (lax.broadcasted_iota(jnp.int32, (length, hd), 1) == 0).astype(BF16)

    gate_vals = []
    for s in range(nseq):
        gates = gt_ref[s]
        lf = _log_sigmoid(gates)
        hi = lf.astype(BF16)
        rem = lf - hi.astype(F32)
        mid = rem.astype(BF16)
        lo = (rem - mid.astype(F32)).astype(BF16)
        bcum = (jnp.dot(tril, hi, preferred_element_type=F32)
                + jnp.dot(tril, mid, preferred_element_type=F32)
                + jnp.dot(tril, lo, preferred_element_type=F32))
        b = pltpu.roll(bcum, LANES - FORGET_COL, 1)
        a = gates - b
        m_prev = m_sc[s][0:1, :]
        m_inter = b + m_prev
        m = jnp.maximum(m_inter, b + _cummax_rows(a))
        w_inter = jnp.exp(m_inter - m)
        u = b - m
        floor = jnp.exp(-m)
        a_t = a.T
        m_end = m[length - 1:length, :]
        b_end = b[length - 1:length, :]
        w_end = jnp.exp(a + (b_end - m_end))
        decay = jnp.exp(b_end + m_prev - m_end)
        m_sc[s] = jnp.broadcast_to(m_end, m_sc.shape[1:])
        gate_vals.append((w_inter, u, floor, a_t, w_end, decay))

    pairs = [(s, h) for s in range(nseq) for h in range(MLSTM_HEADS)]
    lanes = lambda h: slice(hd * h, hd * (h + 1))
    col = lambda h: slice(h, h + 1)
    nt_dims = (((1,), (1,)), ((), ()))
    tn_dims = (((0,), (0,)), ((), ()))
    qk_all = [lax.dot_general(q_ref[s, :, lanes(h)], k_ref[s, :, lanes(h)], nt_dims,
                              preferred_element_type=F32) for s, h in pairs]
    inter_all = [jnp.dot(q_ref[s, :, lanes(h)], c_sc[s, h].astype(BF16), preferred_element_type=F32)
                 for s, h in pairs]
    w_all = []
    for (s, h), qk in zip(pairs, qk_all):
        _, u, _, a_t, _, _ = gate_vals[s]
        w_all.append((jnp.exp(jnp.minimum(u[:, col(h)] + a_t[col(h), :], 0.0))
                      * jnp.where(causal, qk, 0.0)).astype(BF16))
    v_one = {(s, h): jnp.concatenate([v_ref[s, :, lanes(h)], one_col], axis=1) for s, h in pairs}
    intra_all = [jnp.dot(w, v_one[p], preferred_element_type=F32) for p, w in zip(pairs, w_all)]
    for (s, h), inter, intra in zip(pairs, inter_all, intra_all):
        w_inter, _, floor, _, _, _ = gate_vals[s]
        both = w_inter[:, col(h)] * inter + intra
        hh = both[:, :hd] / jnp.maximum(jnp.abs(both[:, hd:hd + 1]), floor[:, col(h)])
        o_ref[s, :, lanes(h)] = _head_out(hh, mo_ref[s, :, lanes(h)], g_ref[:, lanes(h)]).astype(o_ref.dtype)
    for s, h in pairs:
        _, _, _, _, w_end, decay = gate_vals[s]
        kw = (k_ref[s, :, lanes(h)].astype(F32) * w_end[:, col(h)]).astype(BF16)
        c_sc[s, h] = (decay[:, col(h)] * c_sc[s, h]
                      + lax.dot_general(kw, v_one[s, h], tn_dims, preferred_element_type=F32))

    @pl.when(chunk == pl.num_programs(1) - 1)
    def _():
        m_out[...] = m_sc[...]
        for s in range(nseq):
            for h in range(MLSTM_HEADS):
                c_fin = c_sc[s, h]
                c_out[s, h] = c_fin[:, :hd]
                n_out[s, h:h + 1, :] = c_fin[:, hd:].T[0:1, :]


def _mlstm(mq, mk, mv, mo, gates, g, batch, chunk_len):
    m, w = mq.shape
    s_len = m // batch
    nck = s_len // chunk_len
    nh, hd = MLSTM_HEADS, MLSTM_HEAD_DIM
    nseq = 2 if batch % 2 == 0 else 1
    groups = batch // nseq
    seqs = lambda t: t.reshape(groups, nseq, s_len, t.shape[-1])
    row = lambda width: pl.BlockSpec((None, nseq, chunk_len, width), lambda p, c: (p, 0, c, 0))
    state = lambda *dims: pl.BlockSpec((None, nseq) + dims, lambda p, c: (p, 0) + (0,) * len(dims))
    h_out, c_fin, n_fin, m_fin = pl.pallas_call(
        _mlstm_kernel,
        grid=(groups, nck),
        in_specs=[row(w), row(w), row(w), row(w), row(LANES), pl.BlockSpec((1, w), lambda p, c: (0, 0))],
        out_specs=[row(w), state(nh, hd, hd), state(nh, hd), state(8, LANES)],
        out_shape=[jax.ShapeDtypeStruct((groups, nseq, s_len, w), BF16),
                   jax.ShapeDtypeStruct((groups, nseq, nh, hd, hd), F32),
                   jax.ShapeDtypeStruct((groups, nseq, nh, hd), F32),
                   jax.ShapeDtypeStruct((groups, nseq, 8, LANES), F32)],
        scratch_shapes=[pltpu.VMEM((nseq, nh, hd, 2 * hd), F32), pltpu.VMEM((nseq, 8, LANES), F32)],
        compiler_params=_cparams(("parallel", "arbitrary")),
        name="mlstm",
    )(seqs(mq), seqs(mk), seqs(mv), seqs(mo), seqs(gates), g)
    return (h_out.reshape(m, w), c_fin.reshape(batch, nh, hd, hd), n_fin.reshape(batch, nh, hd),
            m_fin.reshape(batch, 8, LANES))


def _out_ffn_kernel(x_ref, a_ref, mm_ref, woa_ref, wom_ref, g_ref, wg_ref, wu_ref, wd_ref, gf_ref, *rest):
    hosted = len(rest) > 2
    cache_in = rest[:CACHE_IN] if hosted else ()
    outs = rest[-2 - CACHE_OUT:-1] if hosted else rest[:1]
    o_ref, cache_out, acc_ref = outs[0], outs[1:], rest[-1]
    x = (x_ref[...] + jnp.dot(a_ref[...], woa_ref[...], preferred_element_type=F32)
         + jnp.dot(mm_ref[...], wom_ref[...], preferred_element_type=F32))
    x = _swiglu_ffn(x, g_ref, wg_ref, wu_ref, wd_ref, acc_ref)
    o_ref[...] = _rms(x, gf_ref[...])
    if cache_in:
        _sample_cache_kernel(*cache_in, *cache_out)


def _out_ffn(x, a, mm, woa, wom, g, wg, wu, wd, gf, tm, cache=None, prev=None):
    m, d = x.shape
    row = lambda width: pl.BlockSpec((tm, width), lambda i: (i, 0))
    consts = (woa, wom, g, wg, wu, wd, gf)
    c_ops, c_in, c_out, c_shape = _hosted_cache(cache, 1, CACHE_PARTS, prev) if cache else ([], [], [], [])
    n_in = 3 + len(consts) + len(c_ops)
    res = pl.pallas_call(
        _out_ffn_kernel,
        grid=(m // tm,),
        in_specs=[row(d), row(a.shape[1]), row(mm.shape[1])] + [_const_spec(c.shape) for c in consts] + c_in,
        out_specs=[row(d)] + c_out,
        out_shape=[jax.ShapeDtypeStruct((m, d), F32)] + c_shape,
        scratch_shapes=[pltpu.VMEM((tm, d), F32)],
        input_output_aliases={n_in - 2: 2, n_in - 1: 3} if cache else {},
        compiler_params=_cparams(("parallel",)),
        name="out_ffn",
    )(x, a, mm, *consts, *c_ops)
    return res if cache else res[0]


def _sample_cache_kernel(q_ref, kn_ref, vn_ref, g_ref, ck_ref, cv_ref, o_ref, ok_ref, ov_ref):
    nh, hd, w = ck_ref.shape
    q, kn, vn = q_ref[...], kn_ref[...], vn_ref[...]
    head_row = lax.broadcasted_iota(jnp.int32, (nh, 1), 0)
    head_lane = lax.broadcasted_iota(jnp.int32, (1, nh), 1)
    dist = w - lax.broadcasted_iota(jnp.int32, (1, w), 1)

    s = jnp.zeros((nh, w), F32)
    s_self = jnp.zeros((nh, 1), F32)
    for h in range(nh):
        qc = q[:, h:h + 1]
        s = jnp.where(head_row == h, jnp.sum(ck_ref[h] * qc, axis=0, keepdims=True), s)
        s_self = jnp.where(head_row == h, jnp.sum(qc * kn[:, h:h + 1], axis=0, keepdims=True), s_self)

    ms, ps, pselfs, ls = [], [], [], []
    for dil in DILATIONS:
        in_branch = ((dist & (dil - 1)) == 0) & (dist <= STEPS * dil)
        sm = jnp.where(in_branch, s, -jnp.inf)
        mb = jnp.maximum(jnp.max(sm, axis=1, keepdims=True), s_self)
        p = jnp.exp2(sm - mb)
        p_self = jnp.exp2(s_self - mb)
        ms.append(mb)
        ps.append(p)
        pselfs.append(p_self)
        ls.append(jnp.sum(p, axis=1, keepdims=True) + p_self)
    m_max = functools.reduce(jnp.maximum, ms)
    ws = [jnp.exp2(mb - m_max) for mb in ms]
    p_all = sum(wb * p for wb, p in zip(ws, ps))
    p_new = sum(wb * p for wb, p in zip(ws, pselfs))
    den = sum(wb * lb for wb, lb in zip(ws, ls))

    y = jnp.zeros((hd, nh), F32)
    for h in range(nh):
        o_h = (jnp.sum(cv_ref[h] * p_all[h:h + 1, :], axis=1, keepdims=True)
               + vn[:, h:h + 1] * p_new[h:h + 1, :])
        y = jnp.where(head_lane == h, o_h / den[h:h + 1, :], y)
    o_ref[...] = y * lax.rsqrt(jnp.mean(y * y, axis=0, keepdims=True) + EPS) * g_ref[...]

    last_lane = lax.broadcasted_iota(jnp.int32, (1, LANES), 1) == LANES - 1
    for c_ref, new, out_ref in ((ck_ref, kn, ok_ref), (cv_ref, vn, ov_ref)):
        for h in range(nh):
            shifted = pltpu.roll(c_ref[h], w - 1, 1)
            out_ref[h, :, :w - LANES] = shifted[:, :w - LANES]
            out_ref[h, :, w - LANES:] = jnp.where(last_lane, new[:, h:h + 1], shifted[:, w - LANES:])


CACHE_PARTS = 2
CACHE_IN = 6
CACHE_OUT = 3


def _hosted_cache(cache, part, nparts, prev=None):
    q_t, kn_t, vn_t, g_t, ck, cv = cache
    b, nh, hd, w = ck.shape
    nloc = nh // nparts
    heads = slice(part * nloc, (part + 1) * nloc)
    tok = pl.BlockSpec((None, hd, nloc), lambda i: (i, 0, 0))
    blk = pl.BlockSpec((None, nloc, hd, w), lambda i: (i, part, 0, 0))
    operands = [q_t[..., heads], kn_t[..., heads], vn_t[..., heads], g_t[:, heads], ck, cv]
    in_specs = [tok, tok, tok, pl.BlockSpec((hd, nloc), lambda i: (0, 0)), blk, blk]
    out_specs = [tok, blk, blk]
    out_shape = [jax.ShapeDtypeStruct((b, hd, nloc), F32)] + [jax.ShapeDtypeStruct(ck.shape, F32)] * 2
    if prev is not None:
        operands += list(prev)
        in_specs += [pl.BlockSpec(memory_space=pl.ANY)] * 2
    return operands, in_specs, out_specs, out_shape


def _sample_cache(q_t, kn_t, vn_t, g_t, cache_k_t, cache_v_t):
    b, nh, hd, w = cache_k_t.shape
    tok = pl.BlockSpec((None, hd, nh), lambda i: (i, 0, 0))
    blk = pl.BlockSpec((None, nh, hd, w), lambda i: (i, 0, 0, 0))
    return pl.pallas_call(
        _sample_cache_kernel,
        grid=(b,),
        in_specs=[tok, tok, tok, pl.BlockSpec((hd, nh), lambda i: (0, 0)), blk, blk],
        out_specs=[tok, blk, blk],
        out_shape=[jax.ShapeDtypeStruct((b, hd, nh), F32)] + [jax.ShapeDtypeStruct(cache_k_t.shape, F32)] * 2,
        compiler_params=_cparams(("parallel",)),
        name="sample_cache",
    )(q_t, kn_t, vn_t, g_t, cache_k_t, cache_v_t)


def _mlstm_step_kernel(q_ref, k_ref, v_ref, mo_ref, gt_ref, g_ref, c_ref, n_ref, m_ref,
                       o_ref, c_out, n_out, m_out):
    bb = q_ref.shape[0]
    hd = MLSTM_HEAD_DIM
    gates = gt_ref[...]
    lane = lax.broadcasted_iota(jnp.int32, (bb, LANES), 1)
    eye = (lax.broadcasted_iota(jnp.int32, (hd, hd), 0)
           == lax.broadcasted_iota(jnp.int32, (hd, hd), 1)).astype(BF16)
    m_all = jnp.zeros((bb, LANES), F32)
    for h in range(MLSTM_HEADS):
        sl = slice(hd * h, hd * (h + 1))
        q, k, v = q_ref[:, sl], k_ref[:, sl], v_ref[:, sl]
        qf, kf, vf = q.astype(F32), k.astype(F32), v.astype(F32)
        ig = gates[:, h:h + 1]
        lf = _log_sigmoid(gates[:, FORGET_COL + h:FORGET_COL + h + 1])
        m_prev = m_ref[:, h:h + 1]
        n_prev = n_ref[:, sl]
        m_inter = lf + m_prev
        m = jnp.maximum(m_inter, ig)
        w_in = jnp.exp(ig - m)
        w_inter = jnp.exp(m_inter - m)
        q_t = lax.dot_general(eye, q, (((1,), (1,)), ((), ())), preferred_element_type=F32)
        k_t = lax.dot_general(eye, k, (((1,), (1,)), ((), ())), preferred_element_type=F32)
        qc_rows = []
        for i in range(bb):
            c_prev = c_ref[0, i, h]
            c_bf = c_prev.astype(BF16).astype(F32)
            qc_rows.append(jnp.sum(q_t[:, i:i + 1] * c_bf, axis=0, keepdims=True))
            c_out[0, i, h] = (w_inter[i:i + 1, :] * c_prev
                              + (w_in[i:i + 1, :] * k_t[:, i:i + 1]) * vf[i:i + 1, :])
        qc = jnp.concatenate(qc_rows, axis=0)
        w_intra = w_in * jnp.sum(qf * kf, axis=1, keepdims=True)
        num = w_inter * qc + w_intra * vf
        den = w_inter * jnp.sum(qf * n_prev, axis=1, keepdims=True) + w_intra
        hh = num / jnp.maximum(jnp.abs(den), jnp.exp(-m))
        o_ref[:, sl] = _head_out(hh, mo_ref[:, sl], g_ref[:, sl]).astype(o_ref.dtype)
        n_out[:, sl] = w_inter * n_prev + w_in * kf
        m_all = jnp.where(lane == h, m, m_all)
    m_out[...] = m_all


def _mlstm_step(mq, mk, mv, mo, gates, g, state_c, state_n, state_m, bb):
    b, w = mq.shape
    nh, hd = MLSTM_HEADS, MLSTM_HEAD_DIM
    row = lambda width: pl.BlockSpec((bb, width), lambda i: (i, 0))
    c_spec = pl.BlockSpec((1, bb, nh, hd, hd), lambda i: (0, i, 0, 0, 0))
    return pl.pallas_call(
        _mlstm_step_kernel,
        grid=(b // bb,),
        in_specs=[row(w), row(w), row(w), row(w), row(LANES), pl.BlockSpec((1, w), lambda i: (0, 0)),
                  c_spec, row(w), row(nh)],
        out_specs=[row(w), c_spec, row(w), row(LANES)],
        out_shape=[jax.ShapeDtypeStruct((b, w), BF16),
                   jax.ShapeDtypeStruct(state_c.shape, F32),
                   jax.ShapeDtypeStruct((b, w), F32),
                   jax.ShapeDtypeStruct((b, LANES), F32)],
        compiler_params=_cparams(("parallel",)),
        name="mlstm_step",
    )(mq, mk, mv, mo, gates, g, state_c, state_n, state_m)


def _rope_tables(pos):
    half = ROT_DIM // 2
    inv = jnp.power(jnp.float32(ROPE_THETA), -jnp.arange(half, dtype=F32) * 2.0 / ROT_DIM)
    ang = pos.astype(F32)[:, None] * inv[None, :]
    cos, sin = jnp.cos(ang), jnp.sin(ang)
    rest = ATTN_HEAD_DIM - ROT_DIM
    one = jnp.ones((pos.shape[0], rest), F32)
    zero = jnp.zeros((pos.shape[0], rest), F32)
    zh = jnp.zeros_like(sin)
    head = lambda parts: jnp.tile(jnp.concatenate(parts, axis=1), (1, LANES // ATTN_HEAD_DIM))
    return head([cos, cos, one]), head([-sin, zh, zero]), head([zh, sin, zero])


def _ffn_weights(g, w_gate, w_up, w_down):
    d, f = w_gate.shape
    assert f % FF_CHUNK == 0
    return g.reshape(1, d), w_gate.astype(BF16), w_up.astype(BF16), w_down.astype(BF16)


def _row_tile(m, want):
    return want if m % want == 0 else m


def kernel(x_prompt, x_sample, cache_k, cache_v, state_C, state_n, state_m, norm_ffn1, w_ffn1_gate,
           w_ffn1_up, w_ffn1_down, norm_mix, w_in, b_gate, g_attn_out, g_mlstm_out, w_out, norm_ffn2,
           w_ffn2_gate, w_ffn2_up, w_ffn2_down, norm_final):
    batch, s_len, d = x_prompt.shape
    db, ds, _ = x_sample.shape
    assert norm_ffn1.shape[0] == 1 and ds == 1
    aw = g_attn_out.shape[1]
    mw = g_mlstm_out.shape[1]
    nh_attn = aw // ATTN_HEAD_DIM
    w_buf = cache_k.shape[2]
    assert w_buf == WINDOW_MAX and w_buf % (max(DILATIONS) * STEPS) == 0

    ffn1 = _ffn_weights(norm_ffn1[0], w_ffn1_gate[0], w_ffn1_up[0], w_ffn1_down[0])
    ffn2 = _ffn_weights(norm_ffn2[0], w_ffn2_gate[0], w_ffn2_up[0], w_ffn2_down[0])
    w_in_bf = w_in[0].astype(BF16)
    ws = [w_in_bf[:, i * aw:(i + 1) * aw] for i in range(7)]
    ngate = 2 * MLSTM_HEADS
    wgt = jnp.pad(w_in_bf[:, 7 * aw:], ((0, 0), (0, LANES - ngate)))
    bg = jnp.pad(b_gate[0], (0, LANES - ngate)).reshape(1, LANES)
    g_mix = norm_mix.reshape(1, d)
    g_attn = g_attn_out.reshape(1, aw)
    g_ml = g_mlstm_out.reshape(1, mw)
    wo = w_out[0].astype(BF16)
    woa, wom = wo[:aw], wo[aw:]
    g_fin = norm_final.reshape(1, d)

    def inproj(x1, pos, tm, **window):
        return _inproj(x1, g_mix, ws, wgt, bg, _rope_tables(pos), tm=tm, **window)

    def trunk_out(x1, a, mm, tm, **hosted):
        return _out_ffn(x1, a, mm, woa, wom, *ffn2, g_fin, tm=tm, **hosted)

    pos_s = jnp.full((db,), PAST_LEN, jnp.int32)
    x1s = _ffn(x_sample.reshape(db, d), *ffn1, tm=db)
    qs, ks, vs, mqs, mks, mvs, mos, gts = inproj(x1s, pos_s, db)
    tok_t = lambda t: jnp.transpose(t.reshape(db, nh_attn, ATTN_HEAD_DIM), (0, 2, 1))
    rows_minor = lambda c: jnp.transpose(c[0], (0, 2, 3, 1))
    cache = (tok_t(qs), tok_t(ks), tok_t(vs), g_attn.reshape(nh_attn, ATTN_HEAD_DIM).T,
             rows_minor(cache_k), rows_minor(cache_v))

    mp = batch * s_len
    tm_p = _row_tile(s_len, 512)
    keep = min(WINDOW_MAX, s_len)
    hosted = mp // tm_p == db and nh_attn % CACHE_PARTS == 0
    if hosted:
        x1, a_lo, k_t, v_t = _ffn(x_prompt.reshape(mp, d), *ffn1, tm=tm_p, cache=cache)
    else:
        x1 = _ffn(x_prompt.reshape(mp, d), *ffn1, tm=tm_p)
    q, k, v, mq, mk, mv, mo, gates, kw_t, vw_t = inproj(x1, jnp.arange(s_len), tm_p, seq_len=s_len, keep=keep)
    a = _attention(q, k, v, g_attn, batch)
    mm, c_p, n_p, m_p = _mlstm(mq, mk, mv, mo, gates, g_ml, batch, _row_tile(s_len, 256))
    if hosted:
        y_prompt, a_hi, k_t, v_t = trunk_out(x1, a, mm, tm_p, cache=cache, prev=(k_t, v_t))
        a_t = jnp.concatenate([a_lo, a_hi], axis=-1)
    else:
        y_prompt = trunk_out(x1, a, mm, tm_p)
        a_t, k_t, v_t = _sample_cache(*cache)
    y_prompt = y_prompt.reshape(batch, s_len, d)
    window = lambda t: jnp.transpose(t.reshape(batch, nh_attn, ATTN_HEAD_DIM, keep), (0, 3, 1, 2))
    k_prompt, v_prompt = window(kw_t), window(vw_t)

    a_s = jnp.transpose(a_t, (0, 2, 1)).reshape(db, aw).astype(BF16)
    k_s, v_s = jnp.transpose(k_t, (0, 3, 1, 2)), jnp.transpose(v_t, (0, 3, 1, 2))
    mm_s, c_s, n_s, m_s = _mlstm_step(mqs, mks, mvs, mos, gts, g_ml, state_C, state_n.reshape(db, mw),
                                      state_m[0], bb=16)
    n_s = n_s.reshape(state_n.shape)
    y_sample = trunk_out(x1s, a_s, mm_s, db).reshape(db, 1, d)

    return (y_prompt, y_sample,
            k_prompt[None], v_prompt[None], c_p[None], n_p[None], m_p[None, :, 0, :MLSTM_HEADS],
            k_s[None], v_s[None], c_s, n_s, m_s[None, :, :MLSTM_HEADS])
```

```python
import functools
import inspect
import math
from typing import Callable, NamedTuple

import jax
import jax.numpy as jnp
from jax import lax
from jax.experimental import pallas as pl
from jax.experimental.pallas import tpu as pltpu

F32 = jnp.float32
BF16 = jnp.bfloat16

EPS = 1e-6
LOG2_E = math.log2(math.e)
ROPE_THETA = 500000.0
ATTN_HEAD_DIM = 64
ROT_DIM = ATTN_HEAD_DIM // 4
MLSTM_HEADS = 4
MLSTM_HEAD_DIM = 128
DILATIONS = (16, 4, 1)
STEPS = 128
WINDOW_MAX = 2048
PAST_LEN = 8192
FORGET_COL = MLSTM_HEADS

LANES = 128
FF_CHUNK = 256
ROW_TILE = 512
MLSTM_CHUNK = 256
MLSTM_SEQS = 2
ATTN_GROUP = 4
ATTN_CLASSES = 2
CACHE_PARTS = 2
VMEM_LIMIT = 56 * 1024 * 1024


class _Part(NamedTuple):
    body: Callable
    operands: tuple
    in_specs: tuple
    out_specs: tuple
    out_shape: tuple
    scratch: tuple = ()
    aliases: tuple = ()
    before: Callable = None
    after: Callable = None


def _run_parts(name, steps, parts):
    counts = [(len(p.operands), len(p.out_shape), len(p.scratch)) for p in parts]

    def kernel(*refs):
        it = iter(refs)
        ins = [[next(it) for _ in range(c[0])] for c in counts]
        outs = [[next(it) for _ in range(c[1])] for c in counts]
        scrs = [[next(it) for _ in range(c[2])] for c in counts]
        for p, i, o, s in zip(parts, ins, outs, scrs):
            if p.before is not None:
                p.before(i, o, s)
        running = [g for g in (p.body(i, o, s) for p, i, o, s in zip(parts, ins, outs, scrs))
                   if inspect.isgenerator(g)]
        while running:
            for g in list(running):
                try:
                    next(g)
                except StopIteration:
                    running.remove(g)
        for p, i, o, s in zip(parts, ins, outs, scrs):
            if p.after is not None:
                p.after(i, o, s)

    in_off = [sum(c[0] for c in counts[:k]) for k in range(len(parts))]
    out_off = [sum(c[1] for c in counts[:k]) for k in range(len(parts))]
    aliases = {in_off[k] + i: out_off[k] + o for k, p in enumerate(parts) for i, o in p.aliases}
    flat = lambda field: [x for p in parts for x in getattr(p, field)]
    res = pl.pallas_call(
        kernel,
        grid=(steps,),
        in_specs=flat("in_specs"),
        out_specs=flat("out_specs"),
        out_shape=flat("out_shape"),
        scratch_shapes=flat("scratch"),
        input_output_aliases=aliases,
        compiler_params=pltpu.CompilerParams(dimension_semantics=("arbitrary",),
                                             vmem_limit_bytes=VMEM_LIMIT),
        name=name,
    )(*flat("operands"))
    return [res[out_off[k]:out_off[k] + counts[k][1]] for k in range(len(parts))]


def _const_spec(shape):
    nd = len(shape)
    return pl.BlockSpec(shape, lambda *_: (0,) * nd, pipeline_mode=pl.Buffered(1))


def _rows(tm, width, blk0=0):
    return pl.BlockSpec((tm, width), lambda i: (i + blk0, 0))


def _rms(x, g):
    return x * lax.rsqrt(jnp.mean(x * x, axis=-1, keepdims=True) + EPS) * g


def _swiglu_ffn(x, g_ref, wg_ref, wu_ref, wd_ref, acc_ref):
    h = _rms(x, g_ref[...]).astype(BF16)
    for c in range(wg_ref.shape[1] // FF_CHUNK):
        cols = slice(c * FF_CHUNK, (c + 1) * FF_CHUNK)
        gt = jnp.dot(h, wg_ref[:, cols], preferred_element_type=F32)
        ut = jnp.dot(h, wu_ref[:, cols], preferred_element_type=F32)
        a = (gt * jax.nn.sigmoid(gt) * ut).astype(BF16)
        part = jnp.dot(a, wd_ref[cols, :], preferred_element_type=F32)
        if c == 0:
            acc_ref[...] = part
        else:
            acc_ref[...] += part
        yield
    return x + 0.5 * acc_ref[...]


def _ffn_body(ins, outs, scratch):
    x_ref, g_ref, wg_ref, wu_ref, wd_ref = ins
    outs[0][...] = yield from _swiglu_ffn(x_ref[...], g_ref, wg_ref, wu_ref, wd_ref, scratch[0])


def _ffn_part(x, g, wg, wu, wd, tm):
    m, d = x.shape
    return m // tm, _Part(
        _ffn_body, (x, g, wg, wu, wd),
        (_rows(tm, d), _const_spec(g.shape), _const_spec(wg.shape), _const_spec(wu.shape),
         _const_spec(wd.shape)),
        (_rows(tm, d),), (jax.ShapeDtypeStruct((m, d), F32),), (pltpu.VMEM((tm, d), F32),))


def _rope(y, cos, s_up, s_dn):
    parts = []
    for i in range(y.shape[1] // LANES):
        ys = y[:, LANES * i:LANES * (i + 1)]
        parts.append(ys * cos + pltpu.roll(ys, LANES - ROT_DIM // 2, 1) * s_up
                     + pltpu.roll(ys, ROT_DIM // 2, 1) * s_dn)
    return jnp.concatenate(parts, axis=1)


def _inproj_body(ins, outs, scratch):
    (x_ref, g_ref, wq_ref, wk_ref, wv_ref, wmq_ref, wmk_ref, wmv_ref, wmo_ref, wgt_ref, bg_ref,
     cos_ref, sup_ref, sdn_ref) = ins[:14]
    q_ref, k_ref, v_ref, mq_ref, mk_ref, mv_ref, mo_ref, gt_ref = outs[:8]
    h = _rms(x_ref[...], g_ref[...]).astype(BF16)
    cos, s_up, s_dn = cos_ref[...], sup_ref[...], sdn_ref[...]

    def proj(w_ref):
        return jnp.dot(h, w_ref[...], preferred_element_type=F32)

    q_ref[...] = _rope(proj(wq_ref), cos, s_up, s_dn) * (ATTN_HEAD_DIM ** -0.5 * LOG2_E)
    yield
    k = _rope(proj(wk_ref), cos, s_up, s_dn)
    k_ref[...] = k
    yield
    v = proj(wv_ref)
    v_ref[...] = v
    if len(outs) > 8:
        outs[8][...] = k.T
        outs[9][...] = v.T
    yield
    mq_ref[...] = proj(wmq_ref).astype(BF16)
    yield
    mk_ref[...] = (proj(wmk_ref) * (MLSTM_HEAD_DIM ** -0.5)).astype(BF16)
    yield
    mv_ref[...] = proj(wmv_ref).astype(BF16)
    yield
    mo_ref[...] = proj(wmo_ref)
    gt_ref[...] = proj(wgt_ref) + bg_ref[...]


def _inproj_part(x, g, ws, wgt, bg, tabs, tm, row0=0, nrows=None, seq_len=None, keep=0, window=None):
    d = x.shape[1]
    nrows = x.shape[0] if nrows is None else nrows
    w = ws[0].shape[1]
    blk0 = row0 // tm
    nt = tabs[0].shape[0] // tm
    tab = pl.BlockSpec((tm, LANES), lambda i: ((i + blk0) % nt, 0))
    out_dt = (F32, F32, F32, BF16, BF16, BF16, F32)
    operands = [x, g, *ws, wgt, bg, *tabs]
    in_specs = ([_rows(tm, d, blk0), _const_spec(g.shape)] + [_const_spec(wi.shape) for wi in ws]
                + [_const_spec(wgt.shape), _const_spec(bg.shape), tab, tab, tab])
    out_specs = [_rows(tm, w)] * 7 + [_rows(tm, LANES)]
    out_shape = ([jax.ShapeDtypeStruct((nrows, w), dt) for dt in out_dt]
                 + [jax.ShapeDtypeStruct((nrows, LANES), F32)])
    aliases = ()
    if keep:
        assert seq_len % tm == 0 and keep % tm == 0 and row0 % seq_len == 0
        tiles_per_seq, first_kept, seq0 = seq_len // tm, (seq_len - keep) // tm, row0 // seq_len
        kv_t = pl.BlockSpec((None, w, tm), lambda i: (i // tiles_per_seq + seq0, 0,
                                                      jnp.maximum(i % tiles_per_seq - first_kept, 0)))
        out_specs += [kv_t, kv_t]
        out_shape += [jax.ShapeDtypeStruct((x.shape[0] // seq_len, w, keep), F32)] * 2
        if window is not None:
            aliases = ((len(operands), 8), (len(operands) + 1, 9))
            operands += list(window)
            in_specs += [pl.BlockSpec(memory_space=pl.ANY)] * 2
    return nrows // tm, _Part(_inproj_body, tuple(operands), tuple(in_specs), tuple(out_specs),
                              tuple(out_shape), (), aliases)


def _attn_kernel(q_ref, k_ref, v_ref, g_ref, o_ref, acc_ref, m_ref, l_ref, bias_ref):
    s_len = q_ref.shape[0]
    blk = STEPS
    lane = lax.broadcasted_iota(jnp.int32, (1, LANES), 1)
    head0 = lane < ATTN_HEAD_DIM
    qi = lax.broadcasted_iota(jnp.int32, (2 * blk, 2 * blk), 0) & (blk - 1)
    kj = lax.broadcasted_iota(jnp.int32, (2 * blk, 2 * blk), 1)
    band = (kj >= qi) & (kj <= qi + blk)
    bias_ref[0] = jnp.where(band & (kj >= blk), 0.0, -jnp.inf)
    bias_ref[1] = jnp.where(band, 0.0, -jnp.inf)

    ones_blk = jnp.ones((2 * blk, LANES), BF16)

    def pick(a):
        return jnp.where(head0, a[:blk], a[blk:])

    def branch(dil, first, last):
        nblk = s_len // (dil * blk)

        rows = lambda start: pl.ds(start, blk, stride=dil) if dil > 1 else pl.ds(start, blk)

        def run_units(units):
            old = [None if first else (m_ref[rows(qs_), :], l_ref[rows(qs_), :], acc_ref[rows(qs_), :])
                   for qs_, _, _ in units]
            loaded = {}

            def block(ref, start):
                key = (id(ref), id(start))
                if key not in loaded:
                    loaded[key] = ref[rows(start), :].astype(BF16)
                return loaded[key]

            cat = lambda parts: parts[0] if len(parts) == 1 else jnp.concatenate(parts, axis=0)
            scores, vals = [], []
            for qstart, kstarts, bias in units:
                q = q_ref[rows(qstart), :]
                kk = cat([block(k_ref, st) for st in kstarts])
                vv = cat([block(v_ref, st) for st in kstarts])
                vals.append(jnp.concatenate([vv, ones_blk[:vv.shape[0]]], axis=1))
                qs = jnp.concatenate([jnp.where(head0, q, 0.0), jnp.where(head0, 0.0, q)],
                                     axis=0).astype(BF16)
                s = lax.dot_general(qs, kk, (((1,), (1,)), ((), ())), preferred_element_type=F32)
                scores.append(s + bias)
            stats = []
            for s in scores:
                mb = jnp.max(s, axis=1, keepdims=True)
                p = jnp.exp2(s - mb)
                stats.append((mb, p.astype(BF16)))
            outs = [jnp.dot(p, vv, preferred_element_type=F32) for (_, p), vv in zip(stats, vals)]
            for (qstart, _, _), prev, (mb, _), res in zip(units, old, stats, outs):
                mb = pick(jnp.broadcast_to(mb, (2 * blk, LANES)))
                ob = pick(res[:, :LANES])
                lb = pick(res[:, LANES:])
                if first:
                    m_new, l_new, acc = mb, lb, ob
                else:
                    m_old, l_old, acc_old = prev
                    m_new = jnp.maximum(m_old, mb)
                    w_old = jnp.exp2(m_old - m_new)
                    w_blk = jnp.exp2(mb - m_new)
                    l_new = l_old * w_old + lb * w_blk
                    acc = acc_old * w_old + ob * w_blk
                if last:
                    y = acc / l_new
                    sq = y * y
                    ms0 = jnp.sum(jnp.where(head0, sq, 0.0), axis=1, keepdims=True)
                    ms1 = jnp.sum(jnp.where(head0, 0.0, sq), axis=1, keepdims=True)
                    ms = jnp.where(head0, ms0, ms1) * (1.0 / ATTN_HEAD_DIM)
                    o_ref[rows(qstart), :] = (y * lax.rsqrt(ms + EPS) * g_ref[...]).astype(o_ref.dtype)
                else:
                    m_ref[rows(qstart), :] = m_new
                    l_ref[rows(qstart), :] = l_new
                    acc_ref[rows(qstart), :] = acc

        def align(start):
            return pl.multiple_of(start, blk) if dil == 1 else start

        if nblk * ATTN_CLASSES <= ATTN_GROUP:
            def classes(ci, carry):
                units = []
                for j in range(ATTN_CLASSES):
                    starts = [align(ci * ATTN_CLASSES + j + dil * blk * n) for n in range(nblk)]
                    units.append((starts[0], starts[:1], bias_ref[1][:, blk:]))
                    units += [(starts[n], starts[n - 1:n + 1], bias_ref[1]) for n in range(1, nblk)]
                run_units(units)
                return carry

            assert dil % ATTN_CLASSES == 0
            lax.fori_loop(0, dil // ATTN_CLASSES, classes, 0)
        else:
            def group(gi, carry):
                units = []
                for j in range(ATTN_GROUP):
                    u = gi * ATTN_GROUP + j
                    r = u // nblk
                    n = u % nblk
                    units.append((align(r + dil * blk * n),
                                  [align(r + dil * blk * jnp.maximum(n - 1, 0)), align(r + dil * blk * n)],
                                  bias_ref[jnp.minimum(n, 1)]))
                run_units(units)
                return carry

            assert (dil * nblk) % ATTN_GROUP == 0
            lax.fori_loop(0, dil * nblk // ATTN_GROUP, group, 0)

    for idx, dil in enumerate(DILATIONS):
        branch(dil, idx == 0, idx == len(DILATIONS) - 1)


def _attention(q, k, v, g, batch):
    m, w = q.shape
    s_len = m // batch
    assert s_len % (max(DILATIONS) * STEPS) == 0
    nslab = w // LANES
    blk = pl.BlockSpec((s_len, LANES), lambda b, j: (b, j))
    return pl.pallas_call(
        _attn_kernel,
        grid=(batch, nslab),
        in_specs=[blk, blk, blk, pl.BlockSpec((1, LANES), lambda b, j: (0, j))],
        out_specs=blk,
        out_shape=jax.ShapeDtypeStruct((m, w), BF16),
        scratch_shapes=[pltpu.VMEM((s_len, LANES), F32)] * 3 + [pltpu.VMEM((2, 2 * STEPS, 2 * STEPS), F32)],
        compiler_params=pltpu.CompilerParams(dimension_semantics=("parallel", "parallel"),
                                             vmem_limit_bytes=VMEM_LIMIT),
        name="attention",
    )(q, k, v, g)


def _log_sigmoid(x):
    return jnp.minimum(x, 0.0) - jnp.log1p(jnp.exp(-jnp.abs(x)))


def _head_out(h, mo, g):
    hm = jax.nn.sigmoid(mo) * h
    return hm * lax.rsqrt(jnp.mean(hm * hm, axis=-1, keepdims=True) + EPS) * g


def _cummax_rows(x):
    row = lax.broadcasted_iota(jnp.int32, x.shape, 0)
    k = 1
    while k < x.shape[0]:
        x = jnp.maximum(x, jnp.where(row >= k, pltpu.roll(x, k, 0), -jnp.inf))
        k *= 2
    return x


def _mlstm_body(ins, outs, scratch):
    q_ref, k_ref, v_ref, mo_ref, gt_ref, g_ref = ins
    o_ref = outs[0]
    c_sc, m_sc = scratch
    nseq, length = q_ref.shape[:2]
    hd = MLSTM_HEAD_DIM

    ti = lax.broadcasted_iota(jnp.int32, (length, length), 0)
    si = lax.broadcasted_iota(jnp.int32, (length, length), 1)
    causal = si <= ti
    tril = causal.astype(BF16)
    one_col = (lax.broadcasted_iota(jnp.int32, (length, hd), 1) == 0).astype(BF16)

    gate_vals = []
    for s in range(nseq):
        gates = gt_ref[s]
        lf = _log_sigmoid(gates)
        hi = lf.astype(BF16)
        rem = lf - hi.astype(F32)
        mid = rem.astype(BF16)
        lo = (rem - mid.astype(F32)).astype(BF16)
        bcum = (jnp.dot(tril, hi, preferred_element_type=F32)
                + jnp.dot(tril, mid, preferred_element_type=F32)
                + jnp.dot(tril, lo, preferred_element_type=F32))
        b = pltpu.roll(bcum, LANES - FORGET_COL, 1)
        a = gates - b
        m_prev = m_sc[s][0:1, :]
        m_inter = b + m_prev
        m = jnp.maximum(m_inter, b + _cummax_rows(a))
        w_inter = jnp.exp(m_inter - m)
        u = b - m
        floor = jnp.exp(-m)
        a_t = a.T
        m_end = m[length - 1:length, :]
        b_end = b[length - 1:length, :]
        w_end = jnp.exp(a + (b_end - m_end))
        decay = jnp.exp(b_end + m_prev - m_end)
        m_sc[s] = jnp.broadcast_to(m_end, m_sc.shape[1:])
        gate_vals.append((w_inter, u, floor, a_t, w_end, decay))
        yield

    pairs = [(s, h) for s in range(nseq) for h in range(MLSTM_HEADS)]
    lanes = lambda h: slice(hd * h, hd * (h + 1))
    col = lambda h: slice(h, h + 1)
    nt_dims = (((1,), (1,)), ((), ()))
    tn_dims = (((0,), (0,)), ((), ()))
    qk_all = [lax.dot_general(q_ref[s, :, lanes(h)], k_ref[s, :, lanes(h)], nt_dims,
                              preferred_element_type=F32) for s, h in pairs]
    yield
    inter_all = [jnp.dot(q_ref[s, :, lanes(h)], c_sc[s, h].astype(BF16), preferred_element_type=F32)
                 for s, h in pairs]
    yield
    w_all = []
    for (s, h), qk in zip(pairs, qk_all):
        _, u, _, a_t, _, _ = gate_vals[s]
        w_all.append((jnp.exp(jnp.minimum(u[:, col(h)] + a_t[col(h), :], 0.0))
                      * jnp.where(causal, qk, 0.0)).astype(BF16))
    v_one = {(s, h): jnp.concatenate([v_ref[s, :, lanes(h)], one_col], axis=1) for s, h in pairs}
    yield
    intra_all = [jnp.dot(w, v_one[p], preferred_element_type=F32) for p, w in zip(pairs, w_all)]
    yield
    for (s, h), inter, intra in zip(pairs, inter_all, intra_all):
        w_inter, _, floor, _, _, _ = gate_vals[s]
        both = w_inter[:, col(h)] * inter + intra
        hh = both[:, :hd] / jnp.maximum(jnp.abs(both[:, hd:hd + 1]), floor[:, col(h)])
        o_ref[s, :, lanes(h)] = _head_out(hh, mo_ref[s, :, lanes(h)], g_ref[:, lanes(h)]).astype(o_ref.dtype)
    yield
    for s, h in pairs:
        _, _, _, _, w_end, decay = gate_vals[s]
        kw = (k_ref[s, :, lanes(h)].astype(F32) * w_end[:, col(h)]).astype(BF16)
        c_sc[s, h] = (decay[:, col(h)] * c_sc[s, h]
                      + lax.dot_general(kw, v_one[s, h], tn_dims, preferred_element_type=F32))


def _mlstm_reset(ins, outs, scratch, *, nck):
    @pl.when(pl.program_id(0) % nck == 0)
    def _():
        for ref in scratch:
            ref[...] = jnp.zeros_like(ref)


def _mlstm_finish(ins, outs, scratch, *, nck):
    _, c_out, n_out, m_out = outs
    c_sc, m_sc = scratch
    hd = MLSTM_HEAD_DIM

    @pl.when(pl.program_id(0) % nck == nck - 1)
    def _():
        m_out[...] = m_sc[...]
        for s in range(c_sc.shape[0]):
            for h in range(MLSTM_HEADS):
                c_fin = c_sc[s, h]
                c_out[s, h] = c_fin[:, :hd]
                n_out[s, h:h + 1, :] = c_fin[:, hd:].T[0:1, :]


def _mlstm_part(mq, mk, mv, mo, gates, g, batch, chunk_len):
    m, w = mq.shape
    s_len = m // batch
    nck = s_len // chunk_len
    nh, hd = MLSTM_HEADS, MLSTM_HEAD_DIM
    nseq = MLSTM_SEQS if batch % MLSTM_SEQS == 0 else 1
    groups = batch // nseq
    seqs = lambda t: t.reshape(groups, nseq, s_len, t.shape[-1])
    row = lambda width: pl.BlockSpec((None, nseq, chunk_len, width), lambda i: (i // nck, 0, i % nck, 0))
    state = lambda *dims: pl.BlockSpec((None, nseq) + dims, lambda i: (i // nck, 0) + (0,) * len(dims))
    return groups * nck, _Part(
        _mlstm_body,
        (seqs(mq), seqs(mk), seqs(mv), seqs(mo), seqs(gates), g),
        (row(w), row(w), row(w), row(w), row(LANES), pl.BlockSpec((1, w), lambda i: (0, 0))),
        (row(w), state(nh, hd, hd), state(nh, hd), state(8, LANES)),
        (jax.ShapeDtypeStruct((groups, nseq, s_len, w), BF16),
         jax.ShapeDtypeStruct((groups, nseq, nh, hd, hd), F32),
         jax.ShapeDtypeStruct((groups, nseq, nh, hd), F32),
         jax.ShapeDtypeStruct((groups, nseq, 8, LANES), F32)),
        (pltpu.VMEM((nseq, nh, hd, 2 * hd), F32), pltpu.VMEM((nseq, 8, LANES), F32)),
        (), functools.partial(_mlstm_reset, nck=nck), functools.partial(_mlstm_finish, nck=nck))


def _mlstm_results(res, batch):
    h_out, c_fin, n_fin, m_fin = res
    nh, hd = MLSTM_HEADS, MLSTM_HEAD_DIM
    return (h_out.reshape(-1, h_out.shape[-1]), c_fin.reshape(batch, nh, hd, hd),
            n_fin.reshape(batch, nh, hd), m_fin.reshape(batch, 8, LANES)[:, 0, :nh])


def _out_ffn_body(ins, outs, scratch):
    x_ref, a_ref, mm_ref, woa_ref, wom_ref, g_ref, wg_ref, wu_ref, wd_ref, gf_ref = ins[:10]
    x = (x_ref[...] + jnp.dot(a_ref[...], woa_ref[...], preferred_element_type=F32)
         + jnp.dot(mm_ref[...], wom_ref[...], preferred_element_type=F32))
    yield
    x = yield from _swiglu_ffn(x, g_ref, wg_ref, wu_ref, wd_ref, scratch[0])
    outs[0][...] = _rms(x, gf_ref[...])


def _out_ffn_part(x, a, mm, consts, tm, row0=0, y_prev=None):
    m, d = x.shape
    blk0 = row0 // tm
    operands = [x, a, mm, *consts]
    in_specs = ([_rows(tm, d, blk0), _rows(tm, a.shape[1]), _rows(tm, mm.shape[1])]
                + [_const_spec(c.shape) for c in consts])
    aliases = ()
    if y_prev is not None:
        aliases = ((len(operands), 0),)
        operands.append(y_prev)
        in_specs.append(pl.BlockSpec(memory_space=pl.ANY))
    return a.shape[0] // tm, _Part(
        _out_ffn_body, tuple(operands), tuple(in_specs), (_rows(tm, d, blk0),),
        (jax.ShapeDtypeStruct((m, d), F32),), (pltpu.VMEM((tm, d), F32),), aliases)


def _sample_cache_body(ins, outs, scratch):
    q_ref, kn_ref, vn_ref, g_ref, ck_ref, cv_ref = ins[:6]
    o_ref, ok_ref, ov_ref = outs
    nh, hd, w = ck_ref.shape
    q, kn, vn = q_ref[...], kn_ref[...], vn_ref[...]
    head_row = lax.broadcasted_iota(jnp.int32, (nh, 1), 0)
    head_lane = lax.broadcasted_iota(jnp.int32, (1, nh), 1)
    dist = w - lax.broadcasted_iota(jnp.int32, (1, w), 1)

    s = jnp.zeros((nh, w), F32)
    s_self = jnp.zeros((nh, 1), F32)
    for h in range(nh):
        qc = q[:, h:h + 1]
        s = jnp.where(head_row == h, jnp.sum(ck_ref[h] * qc, axis=0, keepdims=True), s)
        s_self = jnp.where(head_row == h, jnp.sum(qc * kn[:, h:h + 1], axis=0, keepdims=True), s_self)

    ms, ps, pselfs, ls = [], [], [], []
    for dil in DILATIONS:
        in_branch = ((dist & (dil - 1)) == 0) & (dist <= STEPS * dil)
        sm = jnp.where(in_branch, s, -jnp.inf)
        mb = jnp.maximum(jnp.max(sm, axis=1, keepdims=True), s_self)
        p = jnp.exp2(sm - mb)
        p_self = jnp.exp2(s_self - mb)
        ms.append(mb)
        ps.append(p)
        pselfs.append(p_self)
        ls.append(jnp.sum(p, axis=1, keepdims=True) + p_self)
    m_max = functools.reduce(jnp.maximum, ms)
    ws = [jnp.exp2(mb - m_max) for mb in ms]
    p_all = sum(wb * p for wb, p in zip(ws, ps))
    p_new = sum(wb * p for wb, p in zip(ws, pselfs))
    den = sum(wb * lb for wb, lb in zip(ws, ls))

    y = jnp.zeros((hd, nh), F32)
    for h in range(nh):
        o_h = (jnp.sum(cv_ref[h] * p_all[h:h + 1, :], axis=1, keepdims=True)
               + vn[:, h:h + 1] * p_new[h:h + 1, :])
        y = jnp.where(head_lane == h, o_h / den[h:h + 1, :], y)
    o_ref[...] = y * lax.rsqrt(jnp.mean(y * y, axis=0, keepdims=True) + EPS) * g_ref[...]

    last_lane = lax.broadcasted_iota(jnp.int32, (1, LANES), 1) == LANES - 1
    for c_ref, new, out_ref in ((ck_ref, kn, ok_ref), (cv_ref, vn, ov_ref)):
        for h in range(nh):
            shifted = pltpu.roll(c_ref[h], w - 1, 1)
            out_ref[h, :, :w - LANES] = shifted[:, :w - LANES]
            out_ref[h, :, w - LANES:] = jnp.where(last_lane, new[:, h:h + 1], shifted[:, w - LANES:])


def _cache_part(cache, part=0, nparts=1, seq0=0, nseq=None, prev=None):
    q_t, kn_t, vn_t, g_t, ck, cv = cache
    b, nh, hd, w = ck.shape
    nseq = b if nseq is None else nseq
    nloc = nh // nparts
    heads = slice(part * nloc, (part + 1) * nloc)
    tok = pl.BlockSpec((None, hd, nloc), lambda i: (i + seq0, 0, 0))
    blk = pl.BlockSpec((None, nloc, hd, w), lambda i: (i + seq0, part, 0, 0))
    operands = [q_t[..., heads], kn_t[..., heads], vn_t[..., heads], g_t[:, heads], ck, cv]
    in_specs = [tok, tok, tok, pl.BlockSpec((hd, nloc), lambda i: (0, 0)), blk, blk]
    aliases = ()
    if prev is not None:
        aliases = ((len(operands), 1), (len(operands) + 1, 2))
        operands += list(prev)
        in_specs += [pl.BlockSpec(memory_space=pl.ANY)] * 2
    return nseq, _Part(
        _sample_cache_body, tuple(operands), tuple(in_specs),
        (pl.BlockSpec((None, hd, nloc), lambda i: (i, 0, 0)), blk, blk),
        (jax.ShapeDtypeStruct((nseq, hd, nloc), F32),) + (jax.ShapeDtypeStruct(ck.shape, F32),) * 2,
        (), aliases)


def _mlstm_step_kernel(q_ref, k_ref, v_ref, mo_ref, gt_ref, g_ref, c_ref, n_ref, m_ref,
                       o_ref, c_out, n_out, m_out):
    bb = q_ref.shape[0]
    hd = MLSTM_HEAD_DIM
    gates = gt_ref[...]
    lane = lax.broadcasted_iota(jnp.int32, (bb, LANES), 1)
    eye = (lax.broadcasted_iota(jnp.int32, (hd, hd), 0)
           == lax.broadcasted_iota(jnp.int32, (hd, hd), 1)).astype(BF16)
    m_all = jnp.zeros((bb, LANES), F32)
    for h in range(MLSTM_HEADS):
        sl = slice(hd * h, hd * (h + 1))
        q, k, v = q_ref[:, sl], k_ref[:, sl], v_ref[:, sl]
        qf, kf, vf = q.astype(F32), k.astype(F32), v.astype(F32)
        ig = gates[:, h:h + 1]
        lf = _log_sigmoid(gates[:, FORGET_COL + h:FORGET_COL + h + 1])
        m_prev = m_ref[:, h:h + 1]
        n_prev = n_ref[:, sl]
        m_inter = lf + m_prev
        m = jnp.maximum(m_inter, ig)
        w_in = jnp.exp(ig - m)
        w_inter = jnp.exp(m_inter - m)
        q_t = lax.dot_general(eye, q, (((1,), (1,)), ((), ())), preferred_element_type=F32)
        k_t = lax.dot_general(eye, k, (((1,), (1,)), ((), ())), preferred_element_type=F32)
        qc_rows = []
        for i in range(bb):
            c_prev = c_ref[0, i, h]
            c_bf = c_prev.astype(BF16).astype(F32)
            qc_rows.append(jnp.sum(q_t[:, i:i + 1] * c_bf, axis=0, keepdims=True))
            c_out[0, i, h] = (w_inter[i:i + 1, :] * c_prev
                              + (w_in[i:i + 1, :] * k_t[:, i:i + 1]) * vf[i:i + 1, :])
        qc = jnp.concatenate(qc_rows, axis=0)
        w_intra = w_in * jnp.sum(qf * kf, axis=1, keepdims=True)
        num = w_inter * qc + w_intra * vf
        den = w_inter * jnp.sum(qf * n_prev, axis=1, keepdims=True) + w_intra
        hh = num / jnp.maximum(jnp.abs(den), jnp.exp(-m))
        o_ref[:, sl] = _head_out(hh, mo_ref[:, sl], g_ref[:, sl]).astype(o_ref.dtype)
        n_out[:, sl] = w_inter * n_prev + w_in * kf
        m_all = jnp.where(lane == h, m, m_all)
    m_out[...] = m_all


def _mlstm_step(mq, mk, mv, mo, gates, g, state_c, state_n, state_m, bb):
    b, w = mq.shape
    nh, hd = MLSTM_HEADS, MLSTM_HEAD_DIM
    row = lambda width: pl.BlockSpec((bb, width), lambda i: (i, 0))
    c_spec = pl.BlockSpec((1, bb, nh, hd, hd), lambda i: (0, i, 0, 0, 0))
    return pl.pallas_call(
        _mlstm_step_kernel,
        grid=(b // bb,),
        in_specs=[row(w), row(w), row(w), row(w), row(LANES), pl.BlockSpec((1, w), lambda i: (0, 0)),
                  c_spec, row(w), row(nh)],
        out_specs=[row(w), c_spec, row(w), row(LANES)],
        out_shape=[jax.ShapeDtypeStruct((b, w), BF16),
                   jax.ShapeDtypeStruct(state_c.shape, F32),
                   jax.ShapeDtypeStruct((b, w), F32),
                   jax.ShapeDtypeStruct((b, LANES), F32)],
        compiler_params=pltpu.CompilerParams(dimension_semantics=("parallel",),
                                             vmem_limit_bytes=VMEM_LIMIT),
        name="mlstm_step",
    )(mq, mk, mv, mo, gates, g, state_c, state_n, state_m)


def _rope_tables(pos):
    half = ROT_DIM // 2
    inv = jnp.power(jnp.float32(ROPE_THETA), -jnp.arange(half, dtype=F32) * 2.0 / ROT_DIM)
    ang = pos.astype(F32)[:, None] * inv[None, :]
    cos, sin = jnp.cos(ang), jnp.sin(ang)
    rest = ATTN_HEAD_DIM - ROT_DIM
    one = jnp.ones((pos.shape[0], rest), F32)
    zero = jnp.zeros((pos.shape[0], rest), F32)
    zh = jnp.zeros_like(sin)
    head = lambda parts: jnp.tile(jnp.concatenate(parts, axis=1), (1, LANES // ATTN_HEAD_DIM))
    return head([cos, cos, one]), head([-sin, zh, zero]), head([zh, sin, zero])


def _ffn_weights(g, w_gate, w_up, w_down):
    d, f = w_gate.shape
    assert f % FF_CHUNK == 0
    return g.reshape(1, d), w_gate.astype(BF16), w_up.astype(BF16), w_down.astype(BF16)


def _row_tile(m, want):
    return want if m % want == 0 else m


def _call(name, steps_and_parts):
    steps = {s for s, _ in steps_and_parts}
    assert len(steps) == 1, (name, steps)
    return _run_parts(name, steps.pop(), [p for _, p in steps_and_parts])


def kernel(x_prompt, x_sample, cache_k, cache_v, state_C, state_n, state_m, norm_ffn1, w_ffn1_gate,
           w_ffn1_up, w_ffn1_down, norm_mix, w_in, b_gate, g_attn_out, g_mlstm_out, w_out, norm_ffn2,
           w_ffn2_gate, w_ffn2_up, w_ffn2_down, norm_final):
    batch, s_len, d = x_prompt.shape
    db, ds, _ = x_sample.shape
    assert norm_ffn1.shape[0] == 1 and ds == 1
    aw = g_attn_out.shape[1]
    mw = g_mlstm_out.shape[1]
    nh_attn = aw // ATTN_HEAD_DIM
    w_buf = cache_k.shape[2]
    assert w_buf == WINDOW_MAX and w_buf % (max(DILATIONS) * STEPS) == 0

    ffn1 = _ffn_weights(norm_ffn1[0], w_ffn1_gate[0], w_ffn1_up[0], w_ffn1_down[0])
    ffn2 = _ffn_weights(norm_ffn2[0], w_ffn2_gate[0], w_ffn2_up[0], w_ffn2_down[0])
    w_in_bf = w_in[0].astype(BF16)
    ws = [w_in_bf[:, i * aw:(i + 1) * aw] for i in range(7)]
    ngate = 2 * MLSTM_HEADS
    wgt = jnp.pad(w_in_bf[:, 7 * aw:], ((0, 0), (0, LANES - ngate)))
    bg = jnp.pad(b_gate[0], (0, LANES - ngate)).reshape(1, LANES)
    g_mix = norm_mix.reshape(1, d)
    g_attn = g_attn_out.reshape(1, aw)
    g_ml = g_mlstm_out.reshape(1, mw)
    wo = w_out[0].astype(BF16)
    out_consts = (wo[:aw], wo[aw:], *ffn2, norm_final.reshape(1, d))

    def inproj_part(x1, pos, tm, **kw):
        return _inproj_part(x1, g_mix, ws, wgt, bg, _rope_tables(pos), tm, **kw)

    xs = x_sample.reshape(db, d)
    (x1s,), = _call("ffn", [_ffn_part(xs, *ffn1, tm=db)])
    (qs, ks, vs, mqs, mks, mvs, mos, gts), = _call("inproj", [inproj_part(x1s, jnp.full((db,), PAST_LEN), db)])
    tok_t = lambda t: jnp.transpose(t.reshape(db, nh_attn, ATTN_HEAD_DIM), (0, 2, 1))
    rows_minor = lambda c: jnp.transpose(c[0], (0, 2, 3, 1))
    cache = (tok_t(qs), tok_t(ks), tok_t(vs), g_attn.reshape(nh_attn, ATTN_HEAD_DIM).T,
             rows_minor(cache_k), rows_minor(cache_v))

    mp = batch * s_len
    xp = x_prompt.reshape(mp, d)
    tm = _row_tile(s_len, ROW_TILE)
    chunk = _row_tile(s_len, MLSTM_CHUNK)
    keep = min(WINDOW_MAX, s_len)
    pos_p = jnp.arange(s_len)
    tiles = mp // tm
    half = batch // 2
    hosted = tiles == db and nh_attn % CACHE_PARTS == 0 and CACHE_PARTS == 2
    split = (hosted and batch % (2 * MLSTM_SEQS) == 0
             and (half // MLSTM_SEQS) * (s_len // chunk) == tiles // 2)

    if split:
        rows_h = half * s_len
        (x1,), (a_lo, k_t, v_t) = _call("ffn", [_ffn_part(xp, *ffn1, tm=tm),
                                               _cache_part(cache, 0, CACHE_PARTS)])
        proj_a, (a_hi0, k_t, v_t) = _call("inproj", [
            inproj_part(x1, pos_p, tm, nrows=rows_h, seq_len=s_len, keep=keep),
            _cache_part(cache, 1, CACHE_PARTS, 0, db // 2, prev=(k_t, v_t))])
        mlstm_a = _mlstm_part(*proj_a[3:8], g_ml, half, chunk)
        proj_b, ml_a = _call("inproj", [
            inproj_part(x1, pos_p, tm, row0=rows_h, nrows=rows_h, seq_len=s_len, keep=keep,
                        window=proj_a[8:10]),
            mlstm_a])
        kw_t, vw_t = proj_b[8:10]
        att_a = _attention(*proj_a[:3], g_attn, half)
        att_b = _attention(*proj_b[:3], g_attn, half)
        mm_a, c_a, n_a, m_a = _mlstm_results(ml_a, half)
        (y_full,), ml_b = _call("out_ffn", [_out_ffn_part(x1, att_a, mm_a, out_consts, tm),
                                            _mlstm_part(*proj_b[3:8], g_ml, half, chunk)])
        mm_b, c_b, n_b, m_b = _mlstm_results(ml_b, half)
        (y_prompt,), (a_hi1, k_t, v_t) = _call("out_ffn", [
            _out_ffn_part(x1, att_b, mm_b, out_consts, tm, row0=rows_h, y_prev=y_full),
            _cache_part(cache, 1, CACHE_PARTS, db // 2, db // 2, prev=(k_t, v_t))])
        c_p, n_p, m_p = (jnp.concatenate(pair, axis=0) for pair in ((c_a, c_b), (n_a, n_b), (m_a, m_b)))
        a_t = jnp.concatenate([a_lo, jnp.concatenate([a_hi0, a_hi1], axis=0)], axis=-1)
    else:
        if hosted:
            (x1,), (a_lo, k_t, v_t) = _call("ffn", [_ffn_part(xp, *ffn1, tm=tm),
                                                   _cache_part(cache, 0, CACHE_PARTS)])
        else:
            (x1,), = _call("ffn", [_ffn_part(xp, *ffn1, tm=tm)])
        proj, = _call("inproj", [inproj_part(x1, pos_p, tm, seq_len=s_len, keep=keep)])
        kw_t, vw_t = proj[8:10]
        att = _attention(*proj[:3], g_attn, batch)
        ml, = _call("mlstm", [_mlstm_part(*proj[3:8], g_ml, batch, chunk)])
        mm, c_p, n_p, m_p = _mlstm_results(ml, batch)
        if hosted:
            (y_prompt,), (a_hi, k_t, v_t) = _call("out_ffn", [
                _out_ffn_part(x1, att, mm, out_consts, tm),
                _cache_part(cache, 1, CACHE_PARTS, prev=(k_t, v_t))])
            a_t = jnp.concatenate([a_lo, a_hi], axis=-1)
        else:
            (y_prompt,), = _call("out_ffn", [_out_ffn_part(x1, att, mm, out_consts, tm)])
            (a_t, k_t, v_t), = _call("sample_cache", [_cache_part(cache)])
    y_prompt = y_prompt.reshape(batch, s_len, d)
    window = lambda t: jnp.transpose(t.reshape(batch, nh_attn, ATTN_HEAD_DIM, keep), (0, 3, 1, 2))
    k_prompt, v_prompt = window(kw_t), window(vw_t)

    a_s = jnp.transpose(a_t, (0, 2, 1)).reshape(db, aw).astype(BF16)
    k_s, v_s = jnp.transpose(k_t, (0, 3, 1, 2)), jnp.transpose(v_t, (0, 3, 1, 2))
    mm_s, c_s, n_s, m_s = _mlstm_step(mqs, mks, mvs, mos, gts, g_ml, state_C, state_n.reshape(db, mw),
                                      state_m[0], bb=16)
    n_s = n_s.reshape(state_n.shape)
    (y_sample,), = _call("out_ffn", [_out_ffn_part(x1s, a_s, mm_s, out_consts, db)])

    return (y_prompt, y_sample.reshape(db, 1, d),
            k_prompt[None], v_prompt[None], c_p[None], n_p[None], m_p[None],
            k_s[None], v_s[None], c_s, n_s, m_s[None, :, :MLSTM_HEADS])
```

```python
import functools
import inspect
import math
from typing import Callable, NamedTuple

import jax
import jax.numpy as jnp
from jax import lax
from jax.experimental import pallas as pl
from jax.experimental.pallas import tpu as pltpu

F32 = jnp.float32
BF16 = jnp.bfloat16

EPS = 1e-6
LOG2_E = math.log2(math.e)
ROPE_THETA = 500000.0
ATTN_HEAD_DIM = 64
ROT_DIM = ATTN_HEAD_DIM // 4
MLSTM_HEADS = 4
MLSTM_HEAD_DIM = 128
DILATIONS = (16, 4, 1)
STEPS = 128
WINDOW_MAX = 2048
PAST_LEN = 8192
FORGET_COL = MLSTM_HEADS

LANES = 128
FF_CHUNK = 256
ROW_TILE = 512
MLSTM_CHUNK = 256
MLSTM_SEQS = 2
ATTN_GROUP = 4
ATTN_CLASSES = 2
ATTN_UNROLL = 4
CACHE_PARTS = 2
VMEM_LIMIT = 56 * 1024 * 1024


class _Part(NamedTuple):
    body: Callable
    operands: tuple
    in_specs: tuple
    out_specs: tuple
    out_shape: tuple
    scratch: tuple = ()
    aliases: tuple = ()
    before: Callable = None
    after: Callable = None


def _run_parts(name, steps, parts):
    counts = [(len(p.operands), len(p.out_shape), len(p.scratch)) for p in parts]

    def kernel(*refs):
        it = iter(refs)
        ins = [[next(it) for _ in range(c[0])] for c in counts]
        outs = [[next(it) for _ in range(c[1])] for c in counts]
        scrs = [[next(it) for _ in range(c[2])] for c in counts]
        for p, i, o, s in zip(parts, ins, outs, scrs):
            if p.before is not None:
                p.before(i, o, s)
        running = [g for g in (p.body(i, o, s) for p, i, o, s in zip(parts, ins, outs, scrs))
                   if inspect.isgenerator(g)]
        while running:
            for g in list(running):
                try:
                    next(g)
                except StopIteration:
                    running.remove(g)
        for p, i, o, s in zip(parts, ins, outs, scrs):
            if p.after is not None:
                p.after(i, o, s)

    in_off = [sum(c[0] for c in counts[:k]) for k in range(len(parts))]
    out_off = [sum(c[1] for c in counts[:k]) for k in range(len(parts))]
    aliases = {in_off[k] + i: out_off[k] + o for k, p in enumerate(parts) for i, o in p.aliases}
    flat = lambda field: [x for p in parts for x in getattr(p, field)]
    res = pl.pallas_call(
        kernel,
        grid=(steps,),
        in_specs=flat("in_specs"),
        out_specs=flat("out_specs"),
        out_shape=flat("out_shape"),
        scratch_shapes=flat("scratch"),
        input_output_aliases=aliases,
        compiler_params=pltpu.CompilerParams(dimension_semantics=("arbitrary",),
                                             vmem_limit_bytes=VMEM_LIMIT),
        name=name,
    )(*flat("operands"))
    return [res[out_off[k]:out_off[k] + counts[k][1]] for k in range(len(parts))]


def _const_spec(shape):
    nd = len(shape)
    return pl.BlockSpec(shape, lambda *_: (0,) * nd, pipeline_mode=pl.Buffered(1))


def _rows(tm, width, blk0=0):
    return pl.BlockSpec((tm, width), lambda i: (i + blk0, 0))


def _rms(x, g):
    return x * lax.rsqrt(jnp.mean(x * x, axis=-1, keepdims=True) + EPS) * g


def _swiglu_ffn(x, g_ref, wg_ref, wu_ref, wd_ref, acc_ref):
    h = _rms(x, g_ref[...]).astype(BF16)
    for c in range(wg_ref.shape[1] // FF_CHUNK):
        cols = slice(c * FF_CHUNK, (c + 1) * FF_CHUNK)
        gt = jnp.dot(h, wg_ref[:, cols], preferred_element_type=F32)
        ut = jnp.dot(h, wu_ref[:, cols], preferred_element_type=F32)
        a = (gt * jax.nn.sigmoid(gt) * ut).astype(BF16)
        part = jnp.dot(a, wd_ref[cols, :], preferred_element_type=F32)
        if c == 0:
            acc_ref[...] = part
        else:
            acc_ref[...] += part
        yield
    return x + 0.5 * acc_ref[...]


def _ffn_body(ins, outs, scratch):
    x_ref, g_ref, wg_ref, wu_ref, wd_ref = ins[:5]
    outs[0][...] = yield from _swiglu_ffn(x_ref[...], g_ref, wg_ref, wu_ref, wd_ref, scratch[0])


def _ffn_part(x, g, wg, wu, wd, tm, row0=0, nrows=None, y_prev=None):
    m, d = x.shape
    nrows = m if nrows is None else nrows
    blk0 = row0 // tm
    operands = [x, g, wg, wu, wd]
    in_specs = [_rows(tm, d, blk0), _const_spec(g.shape), _const_spec(wg.shape), _const_spec(wu.shape),
                _const_spec(wd.shape)]
    aliases = ()
    if y_prev is not None:
        aliases = ((len(operands), 0),)
        operands.append(y_prev)
        in_specs.append(pl.BlockSpec(memory_space=pl.ANY))
    return nrows // tm, _Part(
        _ffn_body, tuple(operands), tuple(in_specs), (_rows(tm, d, blk0),),
        (jax.ShapeDtypeStruct((m, d), F32),), (pltpu.VMEM((tm, d), F32),), aliases)


def _rope(y, cos, s_up, s_dn):
    parts = []
    for i in range(y.shape[1] // LANES):
        ys = y[:, LANES * i:LANES * (i + 1)]
        parts.append(ys * cos + pltpu.roll(ys, LANES - ROT_DIM // 2, 1) * s_up
                     + pltpu.roll(ys, ROT_DIM // 2, 1) * s_dn)
    return jnp.concatenate(parts, axis=1)


def _inproj_body(ins, outs, scratch):
    (x_ref, g_ref, wq_ref, wk_ref, wv_ref, wmq_ref, wmk_ref, wmv_ref, wmo_ref, wgt_ref, bg_ref,
     cos_ref, sup_ref, sdn_ref) = ins[:14]
    q_ref, k_ref, v_ref, mq_ref, mk_ref, mv_ref, mo_ref, gt_ref = outs[:8]
    h = _rms(x_ref[...], g_ref[...]).astype(BF16)
    cos, s_up, s_dn = cos_ref[...], sup_ref[...], sdn_ref[...]

    def proj(w_ref):
        return jnp.dot(h, w_ref[...], preferred_element_type=F32)

    q_ref[...] = _rope(proj(wq_ref), cos, s_up, s_dn) * (ATTN_HEAD_DIM ** -0.5 * LOG2_E)
    yield
    k = _rope(proj(wk_ref), cos, s_up, s_dn)
    k_ref[...] = k
    yield
    v = proj(wv_ref)
    v_ref[...] = v
    if len(outs) > 8:
        outs[8][...] = k.T
        outs[9][...] = v.T
    yield
    mq_ref[...] = proj(wmq_ref).astype(BF16)
    yield
    mk_ref[...] = (proj(wmk_ref) * (MLSTM_HEAD_DIM ** -0.5)).astype(BF16)
    yield
    mv_ref[...] = proj(wmv_ref).astype(BF16)
    yield
    mo_ref[...] = proj(wmo_ref)
    gt_ref[...] = proj(wgt_ref) + bg_ref[...]


def _inproj_part(x, g, ws, wgt, bg, tabs, tm, row0=0, nrows=None, seq_len=None, keep=0, window=None):
    d = x.shape[1]
    nrows = x.shape[0] if nrows is None else nrows
    w = ws[0].shape[1]
    blk0 = row0 // tm
    nt = tabs[0].shape[0] // tm
    tab = pl.BlockSpec((tm, LANES), lambda i: ((i + blk0) % nt, 0))
    out_dt = (F32, F32, F32, BF16, BF16, BF16, F32)
    operands = [x, g, *ws, wgt, bg, *tabs]
    in_specs = ([_rows(tm, d, blk0), _const_spec(g.shape)] + [_const_spec(wi.shape) for wi in ws]
                + [_const_spec(wgt.shape), _const_spec(bg.shape), tab, tab, tab])
    out_specs = [_rows(tm, w)] * 7 + [_rows(tm, LANES)]
    out_shape = ([jax.ShapeDtypeStruct((nrows, w), dt) for dt in out_dt]
                 + [jax.ShapeDtypeStruct((nrows, LANES), F32)])
    aliases = ()
    if keep:
        assert seq_len % tm == 0 and keep % tm == 0 and row0 % seq_len == 0
        tiles_per_seq, first_kept, seq0 = seq_len // tm, (seq_len - keep) // tm, row0 // seq_len
        kv_t = pl.BlockSpec((None, w, tm), lambda i: (i // tiles_per_seq + seq0, 0,
                                                      jnp.maximum(i % tiles_per_seq - first_kept, 0)))
        out_specs += [kv_t, kv_t]
        out_shape += [jax.ShapeDtypeStruct((x.shape[0] // seq_len, w, keep), F32)] * 2
        if window is not None:
            aliases = ((len(operands), 8), (len(operands) + 1, 9))
            operands += list(window)
            in_specs += [pl.BlockSpec(memory_space=pl.ANY)] * 2
    return nrows // tm, _Part(_inproj_body, tuple(operands), tuple(in_specs), tuple(out_specs),
                              tuple(out_shape), (), aliases)


def _attn_kernel(q_ref, k_ref, v_ref, g_ref, o_ref, acc_ref, m_ref, l_ref, bias_ref):
    s_len = q_ref.shape[0]
    blk = STEPS
    lane = lax.broadcasted_iota(jnp.int32, (1, LANES), 1)
    head0 = lane < ATTN_HEAD_DIM
    qi = lax.broadcasted_iota(jnp.int32, (2 * blk, 2 * blk), 0) & (blk - 1)
    kj = lax.broadcasted_iota(jnp.int32, (2 * blk, 2 * blk), 1)
    band = (kj >= qi) & (kj <= qi + blk)
    bias_ref[0] = jnp.where(band & (kj >= blk), 0.0, -jnp.inf)
    bias_ref[1] = jnp.where(band, 0.0, -jnp.inf)

    ones_blk = jnp.ones((2 * blk, LANES), BF16)

    def pick(a):
        return jnp.where(head0, a[:blk], a[blk:])

    def branch(dil, first, last):
        nblk = s_len // (dil * blk)

        rows = lambda start: pl.ds(start, blk, stride=dil) if dil > 1 else pl.ds(start, blk)

        def run_units(units):
            old = [None if first else (m_ref[rows(qs_), :], l_ref[rows(qs_), :], acc_ref[rows(qs_), :])
                   for qs_, _, _ in units]
            loaded = {}

            def block(ref, start):
                key = (id(ref), id(start))
                if key not in loaded:
                    loaded[key] = ref[rows(start), :].astype(BF16)
                return loaded[key]

            cat = lambda parts: parts[0] if len(parts) == 1 else jnp.concatenate(parts, axis=0)
            scores, vals = [], []
            for qstart, kstarts, bias in units:
                q = q_ref[rows(qstart), :]
                kk = cat([block(k_ref, st) for st in kstarts])
                vv = cat([block(v_ref, st) for st in kstarts])
                vals.append(jnp.concatenate([vv, ones_blk[:vv.shape[0]]], axis=1))
                qs = jnp.concatenate([jnp.where(head0, q, 0.0), jnp.where(head0, 0.0, q)],
                                     axis=0).astype(BF16)
                s = lax.dot_general(qs, kk, (((1,), (1,)), ((), ())), preferred_element_type=F32)
                scores.append(s + bias)
            stats = []
            for s in scores:
                mb = jnp.max(s, axis=1, keepdims=True)
                p = jnp.exp2(s - mb)
                stats.append((mb, p.astype(BF16)))
            outs = [jnp.dot(p, vv, preferred_element_type=F32) for (_, p), vv in zip(stats, vals)]
            for (qstart, _, _), prev, (mb, _), res in zip(units, old, stats, outs):
                mb = pick(jnp.broadcast_to(mb, (2 * blk, LANES)))
                ob = pick(res[:, :LANES])
                lb = pick(res[:, LANES:])
                if first:
                    m_new, l_new, acc = mb, lb, ob
                else:
                    m_old, l_old, acc_old = prev
                    m_new = jnp.maximum(m_old, mb)
                    w_old = jnp.exp2(m_old - m_new)
                    w_blk = jnp.exp2(mb - m_new)
                    l_new = l_old * w_old + lb * w_blk
                    acc = acc_old * w_old + ob * w_blk
                if last:
                    y = acc / l_new
                    sq = y * y
                    ms0 = jnp.sum(jnp.where(head0, sq, 0.0), axis=1, keepdims=True)
                    ms1 = jnp.sum(jnp.where(head0, 0.0, sq), axis=1, keepdims=True)
                    ms = jnp.where(head0, ms0, ms1) * (1.0 / ATTN_HEAD_DIM)
                    o_ref[rows(qstart), :] = (y * lax.rsqrt(ms + EPS) * g_ref[...]).astype(o_ref.dtype)
                else:
                    m_ref[rows(qstart), :] = m_new
                    l_ref[rows(qstart), :] = l_new
                    acc_ref[rows(qstart), :] = acc

        def align(start):
            return pl.multiple_of(start, blk) if dil == 1 else start

        if nblk * ATTN_CLASSES <= ATTN_GROUP:
            def classes(ci, carry):
                units = []
                for j in range(ATTN_CLASSES):
                    starts = [align(ci * ATTN_CLASSES + j + dil * blk * n) for n in range(nblk)]
                    units.append((starts[0], starts[:1], bias_ref[1][:, blk:]))
                    units += [(starts[n], starts[n - 1:n + 1], bias_ref[1]) for n in range(1, nblk)]
                run_units(units)
                return carry

            assert dil % ATTN_CLASSES == 0
            lax.fori_loop(0, dil // ATTN_CLASSES, classes, 0, unroll=ATTN_UNROLL)
        else:
            def group(gi, carry):
                units = []
                for j in range(ATTN_GROUP):
                    u = gi * ATTN_GROUP + j
                    r = u // nblk
                    n = u % nblk
                    units.append((align(r + dil * blk * n),
                                  [align(r + dil * blk * jnp.maximum(n - 1, 0)), align(r + dil * blk * n)],
                                  bias_ref[jnp.minimum(n, 1)]))
                run_units(units)
                return carry

            assert (dil * nblk) % ATTN_GROUP == 0
            lax.fori_loop(0, dil * nblk // ATTN_GROUP, group, 0, unroll=ATTN_UNROLL)

    for idx, dil in enumerate(DILATIONS):
        branch(dil, idx == 0, idx == len(DILATIONS) - 1)


def _attention(q, k, v, g, batch):
    m, w = q.shape
    s_len = m // batch
    assert s_len % (max(DILATIONS) * STEPS) == 0
    nslab = w // LANES
    blk = pl.BlockSpec((s_len, LANES), lambda b, j: (b, j))
    return pl.pallas_call(
        _attn_kernel,
        grid=(batch, nslab),
        in_specs=[blk, blk, blk, pl.BlockSpec((1, LANES), lambda b, j: (0, j))],
        out_specs=blk,
        out_shape=jax.ShapeDtypeStruct((m, w), BF16),
        scratch_shapes=[pltpu.VMEM((s_len, LANES), F32)] * 3 + [pltpu.VMEM((2, 2 * STEPS, 2 * STEPS), F32)],
        compiler_params=pltpu.CompilerParams(dimension_semantics=("parallel", "parallel"),
                                             vmem_limit_bytes=VMEM_LIMIT),
        name="attention",
    )(q, k, v, g)


def _log_sigmoid(x):
    return jnp.minimum(x, 0.0) - jnp.log1p(jnp.exp(-jnp.abs(x)))


def _head_out(h, mo, g):
    hm = jax.nn.sigmoid(mo) * h
    return hm * lax.rsqrt(jnp.mean(hm * hm, axis=-1, keepdims=True) + EPS) * g


def _cummax_rows(x):
    row = lax.broadcasted_iota(jnp.int32, x.shape, 0)
    k = 1
    while k < x.shape[0]:
        x = jnp.maximum(x, jnp.where(row >= k, pltpu.roll(x, k, 0), -jnp.inf))
        k *= 2
    return x


def _mlstm_body(ins, outs, scratch):
    q_ref, k_ref, v_ref, mo_ref, gt_ref, g_ref = ins
    o_ref = outs[0]
    c_sc, m_sc = scratch
    nseq, length = q_ref.shape[:2]
    hd = MLSTM_HEAD_DIM

    ti = lax.broadcasted_iota(jnp.int32, (length, length), 0)
    si = lax.broadcasted_iota(jnp.int32, (length, length), 1)
    causal = si <= ti
    tril = causal.astype(BF16)
    one_col = (lax.broadcasted_iota(jnp.int32, (length, hd), 1) == 0).astype(BF16)

    gate_vals = []
    for s in range(nseq):
        gates = gt_ref[s]
        lf = _log_sigmoid(gates)
        hi = lf.astype(BF16)
        rem = lf - hi.astype(F32)
        mid = rem.astype(BF16)
        lo = (rem - mid.astype(F32)).astype(BF16)
        bcum = (jnp.dot(tril, hi, preferred_element_type=F32)
                + jnp.dot(tril, mid, preferred_element_type=F32)
                + jnp.dot(tril, lo, preferred_element_type=F32))
        b = pltpu.roll(bcum, LANES - FORGET_COL, 1)
        a = gates - b
        m_prev = m_sc[s][0:1, :]
        m_inter = b + m_prev
        m = jnp.maximum(m_inter, b + _cummax_rows(a))
        w_inter = jnp.exp(m_inter - m)
        u = b - m
        floor = jnp.exp(-m)
        a_t = a.T
        m_end = m[length - 1:length, :]
        b_end = b[length - 1:length, :]
        w_end = jnp.exp(a + (b_end - m_end))
        decay = jnp.exp(b_end + m_prev - m_end)
        m_sc[s] = jnp.broadcast_to(m_end, m_sc.shape[1:])
        gate_vals.append((w_inter, u, floor, a_t, w_end, decay))
        yield

    pairs = [(s, h) for s in range(nseq) for h in range(MLSTM_HEADS)]
    lanes = lambda h: slice(hd * h, hd * (h + 1))
    col = lambda h: slice(h, h + 1)
    nt_dims = (((1,), (1,)), ((), ()))
    tn_dims = (((0,), (0,)), ((), ()))
    qk_all = [lax.dot_general(q_ref[s, :, lanes(h)], k_ref[s, :, lanes(h)], nt_dims,
                              preferred_element_type=F32) for s, h in pairs]
    yield
    inter_all = [jnp.dot(q_ref[s, :, lanes(h)], c_sc[s, h].astype(BF16), preferred_element_type=F32)
                 for s, h in pairs]
    yield
    w_all = []
    for (s, h), qk in zip(pairs, qk_all):
        _, u, _, a_t, _, _ = gate_vals[s]
        w_all.append((jnp.exp(jnp.minimum(u[:, col(h)] + a_t[col(h), :], 0.0))
                      * jnp.where(causal, qk, 0.0)).astype(BF16))
    v_one = {(s, h): jnp.concatenate([v_ref[s, :, lanes(h)], one_col], axis=1) for s, h in pairs}
    yield
    intra_all = [jnp.dot(w, v_one[p], preferred_element_type=F32) for p, w in zip(pairs, w_all)]
    yield
    for (s, h), inter, intra in zip(pairs, inter_all, intra_all):
        w_inter, _, floor, _, _, _ = gate_vals[s]
        both = w_inter[:, col(h)] * inter + intra
        hh = both[:, :hd] / jnp.maximum(jnp.abs(both[:, hd:hd + 1]), floor[:, col(h)])
        o_ref[s, :, lanes(h)] = _head_out(hh, mo_ref[s, :, lanes(h)], g_ref[:, lanes(h)]).astype(o_ref.dtype)
    yield
    for s, h in pairs:
        _, _, _, _, w_end, decay = gate_vals[s]
        kw = (k_ref[s, :, lanes(h)].astype(F32) * w_end[:, col(h)]).astype(BF16)
        c_sc[s, h] = (decay[:, col(h)] * c_sc[s, h]
                      + lax.dot_general(kw, v_one[s, h], tn_dims, preferred_element_type=F32))


def _mlstm_reset(ins, outs, scratch, *, nck):
    @pl.when(pl.program_id(0) % nck == 0)
    def _():
        for ref in scratch:
            ref[...] = jnp.zeros_like(ref)


def _mlstm_finish(ins, outs, scratch, *, nck):
    _, c_out, n_out, m_out = outs
    c_sc, m_sc = scratch
    hd = MLSTM_HEAD_DIM

    @pl.when(pl.program_id(0) % nck == nck - 1)
    def _():
        m_out[...] = m_sc[...]
        for s in range(c_sc.shape[0]):
            for h in range(MLSTM_HEADS):
                c_fin = c_sc[s, h]
                c_out[s, h] = c_fin[:, :hd]
                n_out[s, h:h + 1, :] = c_fin[:, hd:].T[0:1, :]


def _mlstm_part(mq, mk, mv, mo, gates, g, batch, chunk_len):
    m, w = mq.shape
    s_len = m // batch
    nck = s_len // chunk_len
    nh, hd = MLSTM_HEADS, MLSTM_HEAD_DIM
    nseq = MLSTM_SEQS if batch % MLSTM_SEQS == 0 else 1
    groups = batch // nseq
    seqs = lambda t: t.reshape(groups, nseq, s_len, t.shape[-1])
    row = lambda width: pl.BlockSpec((None, nseq, chunk_len, width), lambda i: (i // nck, 0, i % nck, 0))
    state = lambda *dims: pl.BlockSpec((None, nseq) + dims, lambda i: (i // nck, 0) + (0,) * len(dims))
    return groups * nck, _Part(
        _mlstm_body,
        (seqs(mq), seqs(mk), seqs(mv), seqs(mo), seqs(gates), g),
        (row(w), row(w), row(w), row(w), row(LANES), pl.BlockSpec((1, w), lambda i: (0, 0))),
        (row(w), state(nh, hd, hd), state(nh, hd), state(8, LANES)),
        (jax.ShapeDtypeStruct((groups, nseq, s_len, w), BF16),
         jax.ShapeDtypeStruct((groups, nseq, nh, hd, hd), F32),
         jax.ShapeDtypeStruct((groups, nseq, nh, hd), F32),
         jax.ShapeDtypeStruct((groups, nseq, 8, LANES), F32)),
        (pltpu.VMEM((nseq, nh, hd, 2 * hd), F32), pltpu.VMEM((nseq, 8, LANES), F32)),
        (), functools.partial(_mlstm_reset, nck=nck), functools.partial(_mlstm_finish, nck=nck))


def _mlstm_results(res, batch):
    h_out, c_fin, n_fin, m_fin = res
    nh, hd = MLSTM_HEADS, MLSTM_HEAD_DIM
    return (h_out.reshape(-1, h_out.shape[-1]), c_fin.reshape(batch, nh, hd, hd),
            n_fin.reshape(batch, nh, hd), m_fin.reshape(batch, 8, LANES)[:, 0, :nh])


def _out_ffn_body(ins, outs, scratch):
    x_ref, a_ref, mm_ref, woa_ref, wom_ref, g_ref, wg_ref, wu_ref, wd_ref, gf_ref = ins[:10]
    x = (x_ref[...] + jnp.dot(a_ref[...], woa_ref[...], preferred_element_type=F32)
         + jnp.dot(mm_ref[...], wom_ref[...], preferred_element_type=F32))
    yield
    x = yield from _swiglu_ffn(x, g_ref, wg_ref, wu_ref, wd_ref, scratch[0])
    outs[0][...] = _rms(x, gf_ref[...])


def _out_ffn_part(x, a, mm, consts, tm, row0=0, y_prev=None):
    m, d = x.shape
    blk0 = row0 // tm
    operands = [x, a, mm, *consts]
    in_specs = ([_rows(tm, d, blk0), _rows(tm, a.shape[1]), _rows(tm, mm.shape[1])]
                + [_const_spec(c.shape) for c in consts])
    aliases = ()
    if y_prev is not None:
        aliases = ((len(operands), 0),)
        operands.append(y_prev)
        in_specs.append(pl.BlockSpec(memory_space=pl.ANY))
    return a.shape[0] // tm, _Part(
        _out_ffn_body, tuple(operands), tuple(in_specs), (_rows(tm, d, blk0),),
        (jax.ShapeDtypeStruct((m, d), F32),), (pltpu.VMEM((tm, d), F32),), aliases)


def _sample_cache_body(ins, outs, scratch):
    q_ref, kn_ref, vn_ref, g_ref, ck_ref, cv_ref = ins[:6]
    o_ref, ok_ref, ov_ref = outs
    nh, hd, w = ck_ref.shape
    q, kn, vn = q_ref[...], kn_ref[...], vn_ref[...]
    head_row = lax.broadcasted_iota(jnp.int32, (nh, 1), 0)
    head_lane = lax.broadcasted_iota(jnp.int32, (1, nh), 1)
    dist = w - lax.broadcasted_iota(jnp.int32, (1, w), 1)

    s = jnp.zeros((nh, w), F32)
    s_self = jnp.zeros((nh, 1), F32)
    for h in range(nh):
        qc = q[:, h:h + 1]
        s = jnp.where(head_row == h, jnp.sum(ck_ref[h] * qc, axis=0, keepdims=True), s)
        s_self = jnp.where(head_row == h, jnp.sum(qc * kn[:, h:h + 1], axis=0, keepdims=True), s_self)

    ms, ps, pselfs, ls = [], [], [], []
    for dil in DILATIONS:
        in_branch = ((dist & (dil - 1)) == 0) & (dist <= STEPS * dil)
        sm = jnp.where(in_branch, s, -jnp.inf)
        mb = jnp.maximum(jnp.max(sm, axis=1, keepdims=True), s_self)
        p = jnp.exp2(sm - mb)
        p_self = jnp.exp2(s_self - mb)
        ms.append(mb)
        ps.append(p)
        pselfs.append(p_self)
        ls.append(jnp.sum(p, axis=1, keepdims=True) + p_self)
    m_max = functools.reduce(jnp.maximum, ms)
    ws = [jnp.exp2(mb - m_max) for mb in ms]
    p_all = sum(wb * p for wb, p in zip(ws, ps))
    p_new = sum(wb * p for wb, p in zip(ws, pselfs))
    den = sum(wb * lb for wb, lb in zip(ws, ls))

    y = jnp.zeros((hd, nh), F32)
    for h in range(nh):
        o_h = (jnp.sum(cv_ref[h] * p_all[h:h + 1, :], axis=1, keepdims=True)
               + vn[:, h:h + 1] * p_new[h:h + 1, :])
        y = jnp.where(head_lane == h, o_h / den[h:h + 1, :], y)
    o_ref[...] = y * lax.rsqrt(jnp.mean(y * y, axis=0, keepdims=True) + EPS) * g_ref[...]

    last_lane = lax.broadcasted_iota(jnp.int32, (1, LANES), 1) == LANES - 1
    for c_ref, new, out_ref in ((ck_ref, kn, ok_ref), (cv_ref, vn, ov_ref)):
        for h in range(nh):
            shifted = pltpu.roll(c_ref[h], w - 1, 1)
            out_ref[h, :, :w - LANES] = shifted[:, :w - LANES]
            out_ref[h, :, w - LANES:] = jnp.where(last_lane, new[:, h:h + 1], shifted[:, w - LANES:])


def _cache_part(cache, part=0, nparts=1, seq0=0, nseq=None, prev=None):
    q_t, kn_t, vn_t, g_t, ck, cv = cache
    b, nh, hd, w = ck.shape
    nseq = b if nseq is None else nseq
    nloc = nh // nparts
    heads = slice(part * nloc, (part + 1) * nloc)
    tok = pl.BlockSpec((None, hd, nloc), lambda i: (i + seq0, 0, 0))
    blk = pl.BlockSpec((None, nloc, hd, w), lambda i: (i + seq0, part, 0, 0))
    operands = [q_t[..., heads], kn_t[..., heads], vn_t[..., heads], g_t[:, heads], ck, cv]
    in_specs = [tok, tok, tok, pl.BlockSpec((hd, nloc), lambda i: (0, 0)), blk, blk]
    aliases = ()
    if prev is not None:
        aliases = ((len(operands), 1), (len(operands) + 1, 2))
        operands += list(prev)
        in_specs += [pl.BlockSpec(memory_space=pl.ANY)] * 2
    return nseq, _Part(
        _sample_cache_body, tuple(operands), tuple(in_specs),
        (pl.BlockSpec((None, hd, nloc), lambda i: (i, 0, 0)), blk, blk),
        (jax.ShapeDtypeStruct((nseq, hd, nloc), F32),) + (jax.ShapeDtypeStruct(ck.shape, F32),) * 2,
        (), aliases)


def _mlstm_step_kernel(q_ref, k_ref, v_ref, mo_ref, gt_ref, g_ref, c_ref, n_ref, m_ref,
                       o_ref, c_out, n_out, m_out):
    bb = q_ref.shape[0]
    hd = MLSTM_HEAD_DIM
    gates = gt_ref[...]
    lane = lax.broadcasted_iota(jnp.int32, (bb, LANES), 1)
    eye = (lax.broadcasted_iota(jnp.int32, (hd, hd), 0)
           == lax.broadcasted_iota(jnp.int32, (hd, hd), 1)).astype(BF16)
    m_all = jnp.zeros((bb, LANES), F32)
    for h in range(MLSTM_HEADS):
        sl = slice(hd * h, hd * (h + 1))
        q, k, v = q_ref[:, sl], k_ref[:, sl], v_ref[:, sl]
        qf, kf, vf = q.astype(F32), k.astype(F32), v.astype(F32)
        ig = gates[:, h:h + 1]
        lf = _log_sigmoid(gates[:, FORGET_COL + h:FORGET_COL + h + 1])
        m_prev = m_ref[:, h:h + 1]
        n_prev = n_ref[:, sl]
        m_inter = lf + m_prev
        m = jnp.maximum(m_inter, ig)
        w_in = jnp.exp(ig - m)
        w_inter = jnp.exp(m_inter - m)
        q_t = lax.dot_general(eye, q, (((1,), (1,)), ((), ())), preferred_element_type=F32)
        k_t = lax.dot_general(eye, k, (((1,), (1,)), ((), ())), preferred_element_type=F32)
        qc_rows = []
        for i in range(bb):
            c_prev = c_ref[0, i, h]
            c_bf = c_prev.astype(BF16).astype(F32)
            qc_rows.append(jnp.sum(q_t[:, i:i + 1] * c_bf, axis=0, keepdims=True))
            c_out[0, i, h] = (w_inter[i:i + 1, :] * c_prev
                              + (w_in[i:i + 1, :] * k_t[:, i:i + 1]) * vf[i:i + 1, :])
        qc = jnp.concatenate(qc_rows, axis=0)
        w_intra = w_in * jnp.sum(qf * kf, axis=1, keepdims=True)
        num = w_inter * qc + w_intra * vf
        den = w_inter * jnp.sum(qf * n_prev, axis=1, keepdims=True) + w_intra
        hh = num / jnp.maximum(jnp.abs(den), jnp.exp(-m))
        o_ref[:, sl] = _head_out(hh, mo_ref[:, sl], g_ref[:, sl]).astype(o_ref.dtype)
        n_out[:, sl] = w_inter * n_prev + w_in * kf
        m_all = jnp.where(lane == h, m, m_all)
    m_out[...] = m_all


def _mlstm_step(mq, mk, mv, mo, gates, g, state_c, state_n, state_m, bb):
    b, w = mq.shape
    nh, hd = MLSTM_HEADS, MLSTM_HEAD_DIM
    row = lambda width: pl.BlockSpec((bb, width), lambda i: (i, 0))
    c_spec = pl.BlockSpec((1, bb, nh, hd, hd), lambda i: (0, i, 0, 0, 0))
    return pl.pallas_call(
        _mlstm_step_kernel,
        grid=(b // bb,),
        in_specs=[row(w), row(w), row(w), row(w), row(LANES), pl.BlockSpec((1, w), lambda i: (0, 0)),
                  c_spec, row(w), row(nh)],
        out_specs=[row(w), c_spec, row(w), row(LANES)],
        out_shape=[jax.ShapeDtypeStruct((b, w), BF16),
                   jax.ShapeDtypeStruct(state_c.shape, F32),
                   jax.ShapeDtypeStruct((b, w), F32),
                   jax.ShapeDtypeStruct((b, LANES), F32)],
        compiler_params=pltpu.CompilerParams(dimension_semantics=("parallel",),
                                             vmem_limit_bytes=VMEM_LIMIT),
        name="mlstm_step",
    )(mq, mk, mv, mo, gates, g, state_c, state_n, state_m)


def _rope_tables(pos):
    half = ROT_DIM // 2
    inv = jnp.power(jnp.float32(ROPE_THETA), -jnp.arange(half, dtype=F32) * 2.0 / ROT_DIM)
    ang = pos.astype(F32)[:, None] * inv[None, :]
    cos, sin = jnp.cos(ang), jnp.sin(ang)
    rest = ATTN_HEAD_DIM - ROT_DIM
    one = jnp.ones((pos.shape[0], rest), F32)
    zero = jnp.zeros((pos.shape[0], rest), F32)
    zh = jnp.zeros_like(sin)
    head = lambda parts: jnp.tile(jnp.concatenate(parts, axis=1), (1, LANES // ATTN_HEAD_DIM))
    return head([cos, cos, one]), head([-sin, zh, zero]), head([zh, sin, zero])


def _ffn_weights(g, w_gate, w_up, w_down):
    d, f = w_gate.shape
    assert f % FF_CHUNK == 0
    return g.reshape(1, d), w_gate.astype(BF16), w_up.astype(BF16), w_down.astype(BF16)


def _row_tile(m, want):
    return want if m % want == 0 else m


def _call(name, steps_and_parts):
    steps = {s for s, _ in steps_and_parts}
    assert len(steps) == 1, (name, steps)
    return _run_parts(name, steps.pop(), [p for _, p in steps_and_parts])


def kernel(x_prompt, x_sample, cache_k, cache_v, state_C, state_n, state_m, norm_ffn1, w_ffn1_gate,
           w_ffn1_up, w_ffn1_down, norm_mix, w_in, b_gate, g_attn_out, g_mlstm_out, w_out, norm_ffn2,
           w_ffn2_gate, w_ffn2_up, w_ffn2_down, norm_final):
    batch, s_len, d = x_prompt.shape
    db, ds, _ = x_sample.shape
    assert norm_ffn1.shape[0] == 1 and ds == 1
    aw = g_attn_out.shape[1]
    mw = g_mlstm_out.shape[1]
    nh_attn = aw // ATTN_HEAD_DIM
    w_buf = cache_k.shape[2]
    assert w_buf == WINDOW_MAX and w_buf % (max(DILATIONS) * STEPS) == 0

    ffn1 = _ffn_weights(norm_ffn1[0], w_ffn1_gate[0], w_ffn1_up[0], w_ffn1_down[0])
    ffn2 = _ffn_weights(norm_ffn2[0], w_ffn2_gate[0], w_ffn2_up[0], w_ffn2_down[0])
    w_in_bf = w_in[0].astype(BF16)
    ws = [w_in_bf[:, i * aw:(i + 1) * aw] for i in range(7)]
    ngate = 2 * MLSTM_HEADS
    wgt = jnp.pad(w_in_bf[:, 7 * aw:], ((0, 0), (0, LANES - ngate)))
    bg = jnp.pad(b_gate[0], (0, LANES - ngate)).reshape(1, LANES)
    g_mix = norm_mix.reshape(1, d)
    g_attn = g_attn_out.reshape(1, aw)
    g_ml = g_mlstm_out.reshape(1, mw)
    wo = w_out[0].astype(BF16)
    out_consts = (wo[:aw], wo[aw:], *ffn2, norm_final.reshape(1, d))

    def inproj_part(x1, pos, tm, **kw):
        return _inproj_part(x1, g_mix, ws, wgt, bg, _rope_tables(pos), tm, **kw)

    xs = x_sample.reshape(db, d)
    (x1s,), = _call("ffn", [_ffn_part(xs, *ffn1, tm=db)])
    (qs, ks, vs, mqs, mks, mvs, mos, gts), = _call("inproj", [inproj_part(x1s, jnp.full((db,), PAST_LEN), db)])
    tok_t = lambda t: jnp.transpose(t.reshape(db, nh_attn, ATTN_HEAD_DIM), (0, 2, 1))
    rows_minor = lambda c: jnp.transpose(c[0], (0, 2, 3, 1))
    cache = (tok_t(qs), tok_t(ks), tok_t(vs), g_attn.reshape(nh_attn, ATTN_HEAD_DIM).T,
             rows_minor(cache_k), rows_minor(cache_v))

    mp = batch * s_len
    xp = x_prompt.reshape(mp, d)
    tm = _row_tile(s_len, ROW_TILE)
    chunk = _row_tile(s_len, MLSTM_CHUNK)
    keep = min(WINDOW_MAX, s_len)
    pos_p = jnp.arange(s_len)
    tiles = mp // tm
    half = batch // 2
    hosted = tiles == db and nh_attn % CACHE_PARTS == 0 and CACHE_PARTS == 2
    split = (hosted and batch % (2 * MLSTM_SEQS) == 0
             and (half // MLSTM_SEQS) * (s_len // chunk) == tiles // 2)

    if split:
        rows_h = half * s_len
        db_h = db // 2
        (x1,), (a00, k_t, v_t) = _call("ffn", [_ffn_part(xp, *ffn1, tm=tm, nrows=rows_h),
                                               _cache_part(cache, 0, CACHE_PARTS, 0, db_h)])
        proj_a, (a10, k_t, v_t) = _call("inproj", [
            inproj_part(x1, pos_p, tm, nrows=rows_h, seq_len=s_len, keep=keep),
            _cache_part(cache, 1, CACHE_PARTS, 0, db_h, prev=(k_t, v_t))])
        (x1,), ml_a = _call("ffn", [_ffn_part(xp, *ffn1, tm=tm, row0=rows_h, nrows=rows_h, y_prev=x1),
                                    _mlstm_part(*proj_a[3:8], g_ml, half, chunk)])
        proj_b, (a01, k_t, v_t) = _call("inproj", [
            inproj_part(x1, pos_p, tm, row0=rows_h, nrows=rows_h, seq_len=s_len, keep=keep,
                        window=proj_a[8:10]),
            _cache_part(cache, 0, CACHE_PARTS, db_h, db_h, prev=(k_t, v_t))])
        kw_t, vw_t = proj_b[8:10]
        att_a = _attention(*proj_a[:3], g_attn, half)
        att_b = _attention(*proj_b[:3], g_attn, half)
        mm_a, c_a, n_a, m_a = _mlstm_results(ml_a, half)
        (y_full,), ml_b = _call("out_ffn", [_out_ffn_part(x1, att_a, mm_a, out_consts, tm),
                                            _mlstm_part(*proj_b[3:8], g_ml, half, chunk)])
        mm_b, c_b, n_b, m_b = _mlstm_results(ml_b, half)
        (y_prompt,), (a11, k_t, v_t) = _call("out_ffn", [
            _out_ffn_part(x1, att_b, mm_b, out_consts, tm, row0=rows_h, y_prev=y_full),
            _cache_part(cache, 1, CACHE_PARTS, db_h, db_h, prev=(k_t, v_t))])
        c_p, n_p, m_p = (jnp.concatenate(pair, axis=0) for pair in ((c_a, c_b), (n_a, n_b), (m_a, m_b)))
        a_t = jnp.concatenate([jnp.concatenate([a00, a01], axis=0), jnp.concatenate([a10, a11], axis=0)],
                              axis=-1)
    else:
        if hosted:
            (x1,), (a_lo, k_t, v_t) = _call("ffn", [_ffn_part(xp, *ffn1, tm=tm),
                                                   _cache_part(cache, 0, CACHE_PARTS)])
        else:
            (x1,), = _call("ffn", [_ffn_part(xp, *ffn1, tm=tm)])
        proj, = _call("inproj", [inproj_part(x1, pos_p, tm, seq_len=s_len, keep=keep)])
        kw_t, vw_t = proj[8:10]
        att = _attention(*proj[:3], g_attn, batch)
        ml, = _call("mlstm", [_mlstm_part(*proj[3:8], g_ml, batch, chunk)])
        mm, c_p, n_p, m_p = _mlstm_results(ml, batch)
        if hosted:
            (y_prompt,), (a_hi, k_t, v_t) = _call("out_ffn", [
                _out_ffn_part(x1, att, mm, out_consts, tm),
                _cache_part(cache, 1, CACHE_PARTS, prev=(k_t, v_t))])
            a_t = jnp.concatenate([a_lo, a_hi], axis=-1)
        else:
            (y_prompt,), = _call("out_ffn", [_out_ffn_part(x1, att, mm, out_consts, tm)])
            (a_t, k_t, v_t), = _call("sample_cache", [_cache_part(cache)])
    y_prompt = y_prompt.reshape(batch, s_len, d)
    window = lambda t: jnp.transpose(t.reshape(batch, nh_attn, ATTN_HEAD_DIM, keep), (0, 3, 1, 2))
    k_prompt, v_prompt = window(kw_t), window(vw_t)

    a_s = jnp.transpose(a_t, (0, 2, 1)).reshape(db, aw).astype(BF16)
    k_s, v_s = jnp.transpose(k_t, (0, 3, 1, 2)), jnp.transpose(v_t, (0, 3, 1, 2))
    mm_s, c_s, n_s, m_s = _mlstm_step(mqs, mks, mvs, mos, gts, g_ml, state_C, state_n.reshape(db, mw),
                                      state_m[0], bb=16)
    n_s = n_s.reshape(state_n.shape)
    (y_sample,), = _call("out_ffn", [_out_ffn_part(x1s, a_s, mm_s, out_consts, db)])

    return (y_prompt, y_sample.reshape(db, 1, d),
            k_prompt[None], v_prompt[None], c_p[None], n_p[None], m_p[None],
            k_s[None], v_s[None], c_s, n_s, m_s[None, :, :MLSTM_HEADS])
```

```python
import functools
import inspect
import math
from typing import Callable, NamedTuple

import jax
import jax.numpy as jnp
from jax import lax
from jax.experimental import pallas as pl
from jax.experimental.pallas import tpu as pltpu

F32 = jnp.float32
BF16 = jnp.bfloat16

EPS = 1e-6
LOG2_E = math.log2(math.e)
ROPE_THETA = 500000.0
ATTN_HEAD_DIM = 64
ROT_DIM = ATTN_HEAD_DIM // 4
MLSTM_HEADS = 4
MLSTM_HEAD_DIM = 128
DILATIONS = (16, 4, 1)
STEPS = 128
WINDOW_MAX = 2048
PAST_LEN = 8192
FORGET_COL = MLSTM_HEADS

LANES = 128
SUBLANES = 8
MLSTM_STEP_ROWS = 16
FF_CHUNK = 256
ROW_TILE = 512
MLSTM_CHUNK = 256
MLSTM_SEQS = 2
ATTN_GROUP = 4
ATTN_CLASSES = 2
ATTN_UNROLL = 8
CACHE_PARTS = 2
VMEM_LIMIT = 56 * 1024 * 1024


class _Part(NamedTuple):
    body: Callable
    operands: tuple
    in_specs: tuple
    out_specs: tuple
    out_shape: tuple
    scratch: tuple = ()
    aliases: tuple = ()
    before: Callable = None
    after: Callable = None


def _run_parts(name, steps, parts):
    counts = [(len(p.operands), len(p.out_shape), len(p.scratch)) for p in parts]

    def kernel(*refs):
        it = iter(refs)
        ins = [[next(it) for _ in range(c[0])] for c in counts]
        outs = [[next(it) for _ in range(c[1])] for c in counts]
        scrs = [[next(it) for _ in range(c[2])] for c in counts]
        for p, i, o, s in zip(parts, ins, outs, scrs):
            if p.before is not None:
                p.before(i, o, s)
        running = [g for g in (p.body(i, o, s) for p, i, o, s in zip(parts, ins, outs, scrs))
                   if inspect.isgenerator(g)]
        while running:
            for g in list(running):
                try:
                    next(g)
                except StopIteration:
                    running.remove(g)
        for p, i, o, s in zip(parts, ins, outs, scrs):
            if p.after is not None:
                p.after(i, o, s)

    in_off = [sum(c[0] for c in counts[:k]) for k in range(len(parts))]
    out_off = [sum(c[1] for c in counts[:k]) for k in range(len(parts))]
    aliases = {in_off[k] + i: out_off[k] + o for k, p in enumerate(parts) for i, o in p.aliases}
    flat = lambda field: [x for p in parts for x in getattr(p, field)]
    res = pl.pallas_call(
        kernel,
        grid=(steps,),
        in_specs=flat("in_specs"),
        out_specs=flat("out_specs"),
        out_shape=flat("out_shape"),
        scratch_shapes=flat("scratch"),
        input_output_aliases=aliases,
        compiler_params=pltpu.CompilerParams(dimension_semantics=("arbitrary",),
                                             vmem_limit_bytes=VMEM_LIMIT),
        name=name,
    )(*flat("operands"))
    return [res[out_off[k]:out_off[k] + counts[k][1]] for k in range(len(parts))]


def _const_spec(shape):
    nd = len(shape)
    return pl.BlockSpec(shape, lambda *_: (0,) * nd, pipeline_mode=pl.Buffered(1))


def _rows(tm, width, blk0=0):
    return pl.BlockSpec((tm, width), lambda i: (i + blk0, 0))


def _rms(x, g):
    return x * lax.rsqrt(jnp.mean(x * x, axis=-1, keepdims=True) + EPS) * g


def _swiglu_ffn(x, g_ref, wg_ref, wu_ref, wd_ref, acc_ref):
    h = _rms(x, g_ref[...]).astype(BF16)
    for c in range(wg_ref.shape[1] // FF_CHUNK):
        cols = slice(c * FF_CHUNK, (c + 1) * FF_CHUNK)
        gt = jnp.dot(h, wg_ref[:, cols], preferred_element_type=F32)
        ut = jnp.dot(h, wu_ref[:, cols], preferred_element_type=F32)
        a = (gt * jax.nn.sigmoid(gt) * ut).astype(BF16)
        part = jnp.dot(a, wd_ref[cols, :], preferred_element_type=F32)
        if c == 0:
            acc_ref[...] = part
        else:
            acc_ref[...] += part
        yield
    return x + 0.5 * acc_ref[...]


def _ffn_body(ins, outs, scratch):
    x_ref, g_ref, wg_ref, wu_ref, wd_ref = ins[:5]
    outs[0][...] = yield from _swiglu_ffn(x_ref[...], g_ref, wg_ref, wu_ref, wd_ref, scratch[0])


def _ffn_part(x, g, wg, wu, wd, tm, row0=0, nrows=None, y_prev=None):
    m, d = x.shape
    nrows = m if nrows is None else nrows
    blk0 = row0 // tm
    operands = [x, g, wg, wu, wd]
    in_specs = [_rows(tm, d, blk0), _const_spec(g.shape), _const_spec(wg.shape), _const_spec(wu.shape),
                _const_spec(wd.shape)]
    aliases = ()
    if y_prev is not None:
        aliases = ((len(operands), 0),)
        operands.append(y_prev)
        in_specs.append(pl.BlockSpec(memory_space=pl.ANY))
    return nrows // tm, _Part(
        _ffn_body, tuple(operands), tuple(in_specs), (_rows(tm, d, blk0),),
        (jax.ShapeDtypeStruct((m, d), F32),), (pltpu.VMEM((tm, d), F32),), aliases)


def _rope(y, cos, s_up, s_dn):
    parts = []
    for i in range(y.shape[1] // LANES):
        ys = y[:, LANES * i:LANES * (i + 1)]
        parts.append(ys * cos + pltpu.roll(ys, LANES - ROT_DIM // 2, 1) * s_up
                     + pltpu.roll(ys, ROT_DIM // 2, 1) * s_dn)
    return jnp.concatenate(parts, axis=1)


def _inproj_body(ins, outs, scratch):
    (x_ref, g_ref, wq_ref, wk_ref, wv_ref, wmq_ref, wmk_ref, wmv_ref, wmo_ref, wgt_ref, bg_ref,
     cos_ref, sup_ref, sdn_ref) = ins[:14]
    q_ref, k_ref, v_ref, mq_ref, mk_ref, mv_ref, mo_ref, gt_ref = outs[:8]
    h = _rms(x_ref[...], g_ref[...]).astype(BF16)
    cos, s_up, s_dn = cos_ref[...], sup_ref[...], sdn_ref[...]

    def proj(w_ref):
        return jnp.dot(h, w_ref[...], preferred_element_type=F32)

    q_ref[...] = _rope(proj(wq_ref), cos, s_up, s_dn) * (ATTN_HEAD_DIM ** -0.5 * LOG2_E)
    yield
    k = _rope(proj(wk_ref), cos, s_up, s_dn)
    k_ref[...] = k
    yield
    v = proj(wv_ref)
    v_ref[...] = v
    if len(outs) > 8:
        outs[8][...] = k.T
        outs[9][...] = v.T
    yield
    mq_ref[...] = proj(wmq_ref).astype(BF16)
    yield
    mk_ref[...] = (proj(wmk_ref) * (MLSTM_HEAD_DIM ** -0.5)).astype(BF16)
    yield
    mv_ref[...] = proj(wmv_ref).astype(BF16)
    yield
    mo_ref[...] = proj(wmo_ref)
    gt_ref[...] = proj(wgt_ref) + bg_ref[...]


def _inproj_part(x, g, ws, wgt, bg, tabs, tm, row0=0, nrows=None, seq_len=None, keep=0, window=None):
    d = x.shape[1]
    nrows = x.shape[0] if nrows is None else nrows
    w = ws[0].shape[1]
    blk0 = row0 // tm
    nt = tabs[0].shape[0] // tm
    tab = pl.BlockSpec((tm, LANES), lambda i: ((i + blk0) % nt, 0))
    out_dt = (F32, F32, F32, BF16, BF16, BF16, F32)
    operands = [x, g, *ws, wgt, bg, *tabs]
    in_specs = ([_rows(tm, d, blk0), _const_spec(g.shape)] + [_const_spec(wi.shape) for wi in ws]
                + [_const_spec(wgt.shape), _const_spec(bg.shape), tab, tab, tab])
    out_specs = [_rows(tm, w)] * 7 + [_rows(tm, LANES)]
    out_shape = ([jax.ShapeDtypeStruct((nrows, w), dt) for dt in out_dt]
                 + [jax.ShapeDtypeStruct((nrows, LANES), F32)])
    aliases = ()
    if keep:
        assert seq_len % tm == 0 and keep % tm == 0 and row0 % seq_len == 0
        tiles_per_seq, first_kept, seq0 = seq_len // tm, (seq_len - keep) // tm, row0 // seq_len
        kv_t = pl.BlockSpec((None, w, tm), lambda i: (i // tiles_per_seq + seq0, 0,
                                                      jnp.maximum(i % tiles_per_seq - first_kept, 0)))
        out_specs += [kv_t, kv_t]
        out_shape += [jax.ShapeDtypeStruct((x.shape[0] // seq_len, w, keep), F32)] * 2
        if window is not None:
            aliases = ((len(operands), 8), (len(operands) + 1, 9))
            operands += list(window)
            in_specs += [pl.BlockSpec(memory_space=pl.ANY)] * 2
    return nrows // tm, _Part(_inproj_body, tuple(operands), tuple(in_specs), tuple(out_specs),
                              tuple(out_shape), (), aliases)


def _attn_kernel(q_ref, k_ref, v_ref, g_ref, o_ref, acc_ref, m_ref, l_ref, bias_ref):
    s_len = q_ref.shape[0]
    blk = STEPS
    lane = lax.broadcasted_iota(jnp.int32, (1, LANES), 1)
    head0 = lane < ATTN_HEAD_DIM
    qi = lax.broadcasted_iota(jnp.int32, (2 * blk, 2 * blk), 0) & (blk - 1)
    kj = lax.broadcasted_iota(jnp.int32, (2 * blk, 2 * blk), 1)
    band = (kj >= qi) & (kj <= qi + blk)
    bias_ref[0] = jnp.where(band & (kj >= blk), 0.0, -jnp.inf)
    bias_ref[1] = jnp.where(band, 0.0, -jnp.inf)

    ones_blk = jnp.ones((2 * blk, LANES), BF16)

    def pick(a):
        return jnp.where(head0, a[:blk], a[blk:])

    def branch(dil, first, last):
        nblk = s_len // (dil * blk)

        rows = lambda start: pl.ds(start, blk, stride=dil) if dil > 1 else pl.ds(start, blk)

        def run_units(units):
            old = [None if first else (m_ref[rows(qs_), :], l_ref[rows(qs_), :], acc_ref[rows(qs_), :])
                   for qs_, _, _ in units]
            loaded = {}

            def block(ref, start):
                key = (id(ref), id(start))
                if key not in loaded:
                    loaded[key] = ref[rows(start), :].astype(BF16)
                return loaded[key]

            cat = lambda parts: parts[0] if len(parts) == 1 else jnp.concatenate(parts, axis=0)
            scores, vals = [], []
            for qstart, kstarts, bias in units:
                q = q_ref[rows(qstart), :]
                kk = cat([block(k_ref, st) for st in kstarts])
                vv = cat([block(v_ref, st) for st in kstarts])
                vals.append(jnp.concatenate([vv, ones_blk[:vv.shape[0]]], axis=1))
                qs = jnp.concatenate([jnp.where(head0, q, 0.0), jnp.where(head0, 0.0, q)],
                                     axis=0).astype(BF16)
                s = lax.dot_general(qs, kk, (((1,), (1,)), ((), ())), preferred_element_type=F32)
                scores.append(s + bias)
            stats = []
            for s in scores:
                mb = jnp.max(s, axis=1, keepdims=True)
                p = jnp.exp2(s - mb)
                stats.append((mb, p.astype(BF16)))
            outs = [jnp.dot(p, vv, preferred_element_type=F32) for (_, p), vv in zip(stats, vals)]
            for (qstart, _, _), prev, (mb, _), res in zip(units, old, stats, outs):
                mb = pick(jnp.broadcast_to(mb, (2 * blk, LANES)))
                ob = pick(res[:, :LANES])
                lb = pick(res[:, LANES:])
                if first:
                    m_new, l_new, acc = mb, lb, ob
                else:
                    m_old, l_old, acc_old = prev
                    m_new = jnp.maximum(m_old, mb)
                    w_old = jnp.exp2(m_old - m_new)
                    w_blk = jnp.exp2(mb - m_new)
                    l_new = l_old * w_old + lb * w_blk
                    acc = acc_old * w_old + ob * w_blk
                if last:
                    y = acc / l_new
                    sq = y * y
                    ms0 = jnp.sum(jnp.where(head0, sq, 0.0), axis=1, keepdims=True)
                    ms1 = jnp.sum(jnp.where(head0, 0.0, sq), axis=1, keepdims=True)
                    ms = jnp.where(head0, ms0, ms1) * (1.0 / ATTN_HEAD_DIM)
                    o_ref[rows(qstart), :] = (y * lax.rsqrt(ms + EPS) * g_ref[...]).astype(o_ref.dtype)
                else:
                    m_ref[rows(qstart), :] = m_new
                    l_ref[rows(qstart), :] = l_new
                    acc_ref[rows(qstart), :] = acc

        def align(start):
            return pl.multiple_of(start, blk) if dil == 1 else start

        if nblk * ATTN_CLASSES <= ATTN_GROUP:
            def classes(ci, carry):
                units = []
                for j in range(ATTN_CLASSES):
                    starts = [align(ci * ATTN_CLASSES + j + dil * blk * n) for n in range(nblk)]
                    units.append((starts[0], starts[:1], bias_ref[1][:, blk:]))
                    units += [(starts[n], starts[n - 1:n + 1], bias_ref[1]) for n in range(1, nblk)]
                run_units(units)
                return carry

            assert dil % ATTN_CLASSES == 0
            lax.fori_loop(0, dil // ATTN_CLASSES, classes, 0, unroll=ATTN_UNROLL)
        else:
            def group(gi, carry):
                units = []
                for j in range(ATTN_GROUP):
                    u = gi * ATTN_GROUP + j
                    r = u // nblk
                    n = u % nblk
                    units.append((align(r + dil * blk * n),
                                  [align(r + dil * blk * jnp.maximum(n - 1, 0)), align(r + dil * blk * n)],
                                  bias_ref[jnp.minimum(n, 1)]))
                run_units(units)
                return carry

            assert (dil * nblk) % ATTN_GROUP == 0
            lax.fori_loop(0, dil * nblk // ATTN_GROUP, group, 0, unroll=ATTN_UNROLL)

    for idx, dil in enumerate(DILATIONS):
        branch(dil, idx == 0, idx == len(DILATIONS) - 1)


def _attention(q, k, v, g, batch):
    m, w = q.shape
    s_len = m // batch
    assert s_len % (max(DILATIONS) * STEPS) == 0
    nslab = w // LANES
    blk = pl.BlockSpec((s_len, LANES), lambda b, j: (b, j))
    return pl.pallas_call(
        _attn_kernel,
        grid=(batch, nslab),
        in_specs=[blk, blk, blk, pl.BlockSpec((1, LANES), lambda b, j: (0, j))],
        out_specs=blk,
        out_shape=jax.ShapeDtypeStruct((m, w), BF16),
        scratch_shapes=[pltpu.VMEM((s_len, LANES), F32)] * 3 + [pltpu.VMEM((2, 2 * STEPS, 2 * STEPS), F32)],
        compiler_params=pltpu.CompilerParams(dimension_semantics=("parallel", "parallel"),
                                             vmem_limit_bytes=VMEM_LIMIT),
        name="attention",
    )(q, k, v, g)


def _log_sigmoid(x):
    return jnp.minimum(x, 0.0) - jnp.log1p(jnp.exp(-jnp.abs(x)))


def _head_out(h, mo, g):
    hm = jax.nn.sigmoid(mo) * h
    return hm * lax.rsqrt(jnp.mean(hm * hm, axis=-1, keepdims=True) + EPS) * g


def _cummax_rows(x):
    row = lax.broadcasted_iota(jnp.int32, x.shape, 0)
    k = 1
    while k < x.shape[0]:
        x = jnp.maximum(x, jnp.where(row >= k, pltpu.roll(x, k, 0), -jnp.inf))
        k *= 2
    return x


def _mlstm_body(ins, outs, scratch):
    q_ref, k_ref, v_ref, mo_ref, gt_ref, g_ref = ins
    o_ref = outs[0]
    c_sc, m_sc = scratch
    nseq, length = q_ref.shape[:2]
    hd = MLSTM_HEAD_DIM

    ti = lax.broadcasted_iota(jnp.int32, (length, length), 0)
    si = lax.broadcasted_iota(jnp.int32, (length, length), 1)
    causal = si <= ti
    tril = causal.astype(BF16)
    one_col = (lax.broadcasted_iota(jnp.int32, (length, hd), 1) == 0).astype(BF16)

    gate_vals = []
    for s in range(nseq):
        gates = gt_ref[s]
        lf = _log_sigmoid(gates)
        hi = lf.astype(BF16)
        rem = lf - hi.astype(F32)
        mid = rem.astype(BF16)
        lo = (rem - mid.astype(F32)).astype(BF16)
        bcum = (jnp.dot(tril, hi, preferred_element_type=F32)
                + jnp.dot(tril, mid, preferred_element_type=F32)
                + jnp.dot(tril, lo, preferred_element_type=F32))
        b = pltpu.roll(bcum, LANES - FORGET_COL, 1)
        a = gates - b
        m_prev = m_sc[s][0:1, :]
        m_inter = b + m_prev
        m = jnp.maximum(m_inter, b + _cummax_rows(a))
        w_inter = jnp.exp(m_inter - m)
        u = b - m
        floor = jnp.exp(-m)
        a_t = a.T
        m_end = m[length - 1:length, :]
        b_end = b[length - 1:length, :]
        w_end = jnp.exp(a + (b_end - m_end))
        decay = jnp.exp(b_end + m_prev - m_end)
        m_sc[s] = jnp.broadcast_to(m_end, m_sc.shape[1:])
        gate_vals.append((w_inter, u, floor, a_t, w_end, decay))
        yield

    pairs = [(s, h) for s in range(nseq) for h in range(MLSTM_HEADS)]
    lanes = lambda h: slice(hd * h, hd * (h + 1))
    col = lambda h: slice(h, h + 1)
    nt_dims = (((1,), (1,)), ((), ()))
    tn_dims = (((0,), (0,)), ((), ()))
    qk_all = [lax.dot_general(q_ref[s, :, lanes(h)], k_ref[s, :, lanes(h)], nt_dims,
                              preferred_element_type=F32) for s, h in pairs]
    yield
    inter_all = [jnp.dot(q_ref[s, :, lanes(h)], c_sc[s, h].astype(BF16), preferred_element_type=F32)
                 for s, h in pairs]
    yield
    w_all = []
    for (s, h), qk in zip(pairs, qk_all):
        _, u, _, a_t, _, _ = gate_vals[s]
        w_all.append((jnp.exp(jnp.minimum(u[:, col(h)] + a_t[col(h), :], 0.0))
                      * jnp.where(causal, qk, 0.0)).astype(BF16))
    v_one = {(s, h): jnp.concatenate([v_ref[s, :, lanes(h)], one_col], axis=1) for s, h in pairs}
    yield
    intra_all = [jnp.dot(w, v_one[p], preferred_element_type=F32) for p, w in zip(pairs, w_all)]
    yield
    for (s, h), inter, intra in zip(pairs, inter_all, intra_all):
        w_inter, _, floor, _, _, _ = gate_vals[s]
        both = w_inter[:, col(h)] * inter + intra
        hh = both[:, :hd] / jnp.maximum(jnp.abs(both[:, hd:hd + 1]), floor[:, col(h)])
        o_ref[s, :, lanes(h)] = _head_out(hh, mo_ref[s, :, lanes(h)], g_ref[:, lanes(h)]).astype(o_ref.dtype)
    yield
    for s, h in pairs:
        _, _, _, _, w_end, decay = gate_vals[s]
        kw = (k_ref[s, :, lanes(h)].astype(F32) * w_end[:, col(h)]).astype(BF16)
        c_sc[s, h] = (decay[:, col(h)] * c_sc[s, h]
                      + lax.dot_general(kw, v_one[s, h], tn_dims, preferred_element_type=F32))


def _mlstm_reset(ins, outs, scratch, *, nck):
    @pl.when(pl.program_id(0) % nck == 0)
    def _():
        for ref in scratch:
            ref[...] = jnp.zeros_like(ref)


def _mlstm_finish(ins, outs, scratch, *, nck):
    _, c_out, n_out, m_out = outs
    c_sc, m_sc = scratch
    hd = MLSTM_HEAD_DIM

    @pl.when(pl.program_id(0) % nck == nck - 1)
    def _():
        m_out[...] = m_sc[...]
        for s in range(c_sc.shape[0]):
            for h in range(MLSTM_HEADS):
                c_fin = c_sc[s, h]
                c_out[s, h] = c_fin[:, :hd]
                n_out[s, h:h + 1, :] = c_fin[:, hd:].T[0:1, :]


def _mlstm_part(mq, mk, mv, mo, gates, g, batch, chunk_len):
    m, w = mq.shape
    s_len = m // batch
    nck = s_len // chunk_len
    nh, hd = MLSTM_HEADS, MLSTM_HEAD_DIM
    nseq = MLSTM_SEQS if batch % MLSTM_SEQS == 0 else 1
    groups = batch // nseq
    seqs = lambda t: t.reshape(groups, nseq, s_len, t.shape[-1])
    row = lambda width: pl.BlockSpec((None, nseq, chunk_len, width), lambda i: (i // nck, 0, i % nck, 0))
    state = lambda *dims: pl.BlockSpec((None, nseq) + dims, lambda i: (i // nck, 0) + (0,) * len(dims))
    return groups * nck, _Part(
        _mlstm_body,
        (seqs(mq), seqs(mk), seqs(mv), seqs(mo), seqs(gates), g),
        (row(w), row(w), row(w), row(w), row(LANES), pl.BlockSpec((1, w), lambda i: (0, 0))),
        (row(w), state(nh, hd, hd), state(nh, hd), state(SUBLANES, LANES)),
        (jax.ShapeDtypeStruct((groups, nseq, s_len, w), BF16),
         jax.ShapeDtypeStruct((groups, nseq, nh, hd, hd), F32),
         jax.ShapeDtypeStruct((groups, nseq, nh, hd), F32),
         jax.ShapeDtypeStruct((groups, nseq, SUBLANES, LANES), F32)),
        (pltpu.VMEM((nseq, nh, hd, 2 * hd), F32), pltpu.VMEM((nseq, SUBLANES, LANES), F32)),
        (), functools.partial(_mlstm_reset, nck=nck), functools.partial(_mlstm_finish, nck=nck))


def _mlstm_results(res, batch):
    h_out, c_fin, n_fin, m_fin = res
    nh, hd = MLSTM_HEADS, MLSTM_HEAD_DIM
    return (h_out.reshape(-1, h_out.shape[-1]), c_fin.reshape(batch, nh, hd, hd),
            n_fin.reshape(batch, nh, hd), m_fin.reshape(batch, SUBLANES, LANES)[:, 0, :nh])


def _out_ffn_body(ins, outs, scratch):
    x_ref, a_ref, mm_ref, woa_ref, wom_ref, g_ref, wg_ref, wu_ref, wd_ref, gf_ref = ins[:10]
    x = (x_ref[...] + jnp.dot(a_ref[...], woa_ref[...], preferred_element_type=F32)
         + jnp.dot(mm_ref[...], wom_ref[...], preferred_element_type=F32))
    yield
    x = yield from _swiglu_ffn(x, g_ref, wg_ref, wu_ref, wd_ref, scratch[0])
    outs[0][...] = _rms(x, gf_ref[...])


def _out_ffn_part(x, a, mm, consts, tm, row0=0, y_prev=None):
    m, d = x.shape
    blk0 = row0 // tm
    operands = [x, a, mm, *consts]
    in_specs = ([_rows(tm, d, blk0), _rows(tm, a.shape[1]), _rows(tm, mm.shape[1])]
                + [_const_spec(c.shape) for c in consts])
    aliases = ()
    if y_prev is not None:
        aliases = ((len(operands), 0),)
        operands.append(y_prev)
        in_specs.append(pl.BlockSpec(memory_space=pl.ANY))
    return a.shape[0] // tm, _Part(
        _out_ffn_body, tuple(operands), tuple(in_specs), (_rows(tm, d, blk0),),
        (jax.ShapeDtypeStruct((m, d), F32),), (pltpu.VMEM((tm, d), F32),), aliases)


def _sample_cache_body(ins, outs, scratch):
    q_ref, kn_ref, vn_ref, g_ref, ck_ref, cv_ref = ins[:6]
    o_ref, ok_ref, ov_ref = outs
    nh, hd, w = ck_ref.shape
    q, kn, vn = q_ref[...], kn_ref[...], vn_ref[...]
    head_row = lax.broadcasted_iota(jnp.int32, (nh, 1), 0)
    head_lane = lax.broadcasted_iota(jnp.int32, (1, nh), 1)
    dist = w - lax.broadcasted_iota(jnp.int32, (1, w), 1)

    s = jnp.zeros((nh, w), F32)
    s_self = jnp.zeros((nh, 1), F32)
    for h in range(nh):
        qc = q[:, h:h + 1]
        s = jnp.where(head_row == h, jnp.sum(ck_ref[h] * qc, axis=0, keepdims=True), s)
        s_self = jnp.where(head_row == h, jnp.sum(qc * kn[:, h:h + 1], axis=0, keepdims=True), s_self)

    ms, ps, pselfs, ls = [], [], [], []
    for dil in DILATIONS:
        in_branch = ((dist & (dil - 1)) == 0) & (dist <= STEPS * dil)
        sm = jnp.where(in_branch, s, -jnp.inf)
        mb = jnp.maximum(jnp.max(sm, axis=1, keepdims=True), s_self)
        p = jnp.exp2(sm - mb)
        p_self = jnp.exp2(s_self - mb)
        ms.append(mb)
        ps.append(p)
        pselfs.append(p_self)
        ls.append(jnp.sum(p, axis=1, keepdims=True) + p_self)
    m_max = functools.reduce(jnp.maximum, ms)
    ws = [jnp.exp2(mb - m_max) for mb in ms]
    p_all = sum(wb * p for wb, p in zip(ws, ps))
    p_new = sum(wb * p for wb, p in zip(ws, pselfs))
    den = sum(wb * lb for wb, lb in zip(ws, ls))

    y = jnp.zeros((hd, nh), F32)
    for h in range(nh):
        o_h = (jnp.sum(cv_ref[h] * p_all[h:h + 1, :], axis=1, keepdims=True)
               + vn[:, h:h + 1] * p_new[h:h + 1, :])
        y = jnp.where(head_lane == h, o_h / den[h:h + 1, :], y)
    o_ref[...] = y * lax.rsqrt(jnp.mean(y * y, axis=0, keepdims=True) + EPS) * g_ref[...]

    last_lane = lax.broadcasted_iota(jnp.int32, (1, LANES), 1) == LANES - 1
    for c_ref, new, out_ref in ((ck_ref, kn, ok_ref), (cv_ref, vn, ov_ref)):
        for h in range(nh):
            shifted = pltpu.roll(c_ref[h], w - 1, 1)
            out_ref[h, :, :w - LANES] = shifted[:, :w - LANES]
            out_ref[h, :, w - LANES:] = jnp.where(last_lane, new[:, h:h + 1], shifted[:, w - LANES:])


def _cache_part(cache, part=0, nparts=1, seq0=0, nseq=None, prev=None):
    q_t, kn_t, vn_t, g_t, ck, cv = cache
    b, nh, hd, w = ck.shape
    nseq = b if nseq is None else nseq
    nloc = nh // nparts
    heads = slice(part * nloc, (part + 1) * nloc)
    tok = pl.BlockSpec((None, hd, nloc), lambda i: (i + seq0, 0, 0))
    blk = pl.BlockSpec((None, nloc, hd, w), lambda i: (i + seq0, part, 0, 0))
    operands = [q_t[..., heads], kn_t[..., heads], vn_t[..., heads], g_t[:, heads], ck, cv]
    in_specs = [tok, tok, tok, pl.BlockSpec((hd, nloc), lambda i: (0, 0)), blk, blk]
    aliases = ()
    if prev is not None:
        aliases = ((len(operands), 1), (len(operands) + 1, 2))
        operands += list(prev)
        in_specs += [pl.BlockSpec(memory_space=pl.ANY)] * 2
    return nseq, _Part(
        _sample_cache_body, tuple(operands), tuple(in_specs),
        (pl.BlockSpec((None, hd, nloc), lambda i: (i, 0, 0)), blk, blk),
        (jax.ShapeDtypeStruct((nseq, hd, nloc), F32),) + (jax.ShapeDtypeStruct(ck.shape, F32),) * 2,
        (), aliases)


def _mlstm_step_kernel(q_ref, k_ref, v_ref, mo_ref, gt_ref, g_ref, c_ref, n_ref, m_ref,
                       o_ref, c_out, n_out, m_out):
    bb = q_ref.shape[0]
    hd = MLSTM_HEAD_DIM
    gates = gt_ref[...]
    lane = lax.broadcasted_iota(jnp.int32, (bb, LANES), 1)
    eye = (lax.broadcasted_iota(jnp.int32, (hd, hd), 0)
           == lax.broadcasted_iota(jnp.int32, (hd, hd), 1)).astype(BF16)
    m_all = jnp.zeros((bb, LANES), F32)
    for h in range(MLSTM_HEADS):
        sl = slice(hd * h, hd * (h + 1))
        q, k, v = q_ref[:, sl], k_ref[:, sl], v_ref[:, sl]
        qf, kf, vf = q.astype(F32), k.astype(F32), v.astype(F32)
        ig = gates[:, h:h + 1]
        lf = _log_sigmoid(gates[:, FORGET_COL + h:FORGET_COL + h + 1])
        m_prev = m_ref[:, h:h + 1]
        n_prev = n_ref[:, sl]
        m_inter = lf + m_prev
        m = jnp.maximum(m_inter, ig)
        w_in = jnp.exp(ig - m)
        w_inter = jnp.exp(m_inter - m)
        q_t = lax.dot_general(eye, q, (((1,), (1,)), ((), ())), preferred_element_type=F32)
        k_t = lax.dot_general(eye, k, (((1,), (1,)), ((), ())), preferred_element_type=F32)
        qc_rows = []
        for i in range(bb):
            c_prev = c_ref[0, i, h]
            c_bf = c_prev.astype(BF16).astype(F32)
            qc_rows.append(jnp.sum(q_t[:, i:i + 1] * c_bf, axis=0, keepdims=True))
            c_out[0, i, h] = (w_inter[i:i + 1, :] * c_prev
                              + (w_in[i:i + 1, :] * k_t[:, i:i + 1]) * vf[i:i + 1, :])
        qc = jnp.concatenate(qc_rows, axis=0)
        w_intra = w_in * jnp.sum(qf * kf, axis=1, keepdims=True)
        num = w_inter * qc + w_intra * vf
        den = w_inter * jnp.sum(qf * n_prev, axis=1, keepdims=True) + w_intra
        hh = num / jnp.maximum(jnp.abs(den), jnp.exp(-m))
        o_ref[:, sl] = _head_out(hh, mo_ref[:, sl], g_ref[:, sl]).astype(o_ref.dtype)
        n_out[:, sl] = w_inter * n_prev + w_in * kf
        m_all = jnp.where(lane == h, m, m_all)
    m_out[...] = m_all


def _mlstm_step(mq, mk, mv, mo, gates, g, state_c, state_n, state_m, bb):
    b, w = mq.shape
    nh, hd = MLSTM_HEADS, MLSTM_HEAD_DIM
    row = lambda width: pl.BlockSpec((bb, width), lambda i: (i, 0))
    c_spec = pl.BlockSpec((1, bb, nh, hd, hd), lambda i: (0, i, 0, 0, 0))
    return pl.pallas_call(
        _mlstm_step_kernel,
        grid=(b // bb,),
        in_specs=[row(w), row(w), row(w), row(w), row(LANES), pl.BlockSpec((1, w), lambda i: (0, 0)),
                  c_spec, row(w), row(nh)],
        out_specs=[row(w), c_spec, row(w), row(LANES)],
        out_shape=[jax.ShapeDtypeStruct((b, w), BF16),
                   jax.ShapeDtypeStruct(state_c.shape, F32),
                   jax.ShapeDtypeStruct((b, w), F32),
                   jax.ShapeDtypeStruct((b, LANES), F32)],
        compiler_params=pltpu.CompilerParams(dimension_semantics=("parallel",),
                                             vmem_limit_bytes=VMEM_LIMIT),
        name="mlstm_step",
    )(mq, mk, mv, mo, gates, g, state_c, state_n, state_m)


def _rope_tables(pos):
    half = ROT_DIM // 2
    inv = jnp.power(jnp.float32(ROPE_THETA), -jnp.arange(half, dtype=F32) * 2.0 / ROT_DIM)
    ang = pos.astype(F32)[:, None] * inv[None, :]
    cos, sin = jnp.cos(ang), jnp.sin(ang)
    rest = ATTN_HEAD_DIM - ROT_DIM
    one = jnp.ones((pos.shape[0], rest), F32)
    zero = jnp.zeros((pos.shape[0], rest), F32)
    zh = jnp.zeros_like(sin)
    head = lambda parts: jnp.tile(jnp.concatenate(parts, axis=1), (1, LANES // ATTN_HEAD_DIM))
    return head([cos, cos, one]), head([-sin, zh, zero]), head([zh, sin, zero])


def _ffn_weights(g, w_gate, w_up, w_down):
    d, f = w_gate.shape
    assert f % FF_CHUNK == 0
    return g.reshape(1, d), w_gate.astype(BF16), w_up.astype(BF16), w_down.astype(BF16)


def _row_tile(m, want):
    return want if m % want == 0 else m


def _call(name, steps_and_parts):
    steps = {s for s, _ in steps_and_parts}
    assert len(steps) == 1, (name, steps)
    return _run_parts(name, steps.pop(), [p for _, p in steps_and_parts])


def kernel(x_prompt, x_sample, cache_k, cache_v, state_C, state_n, state_m, norm_ffn1, w_ffn1_gate,
           w_ffn1_up, w_ffn1_down, norm_mix, w_in, b_gate, g_attn_out, g_mlstm_out, w_out, norm_ffn2,
           w_ffn2_gate, w_ffn2_up, w_ffn2_down, norm_final):
    batch, s_len, d = x_prompt.shape
    db, ds, _ = x_sample.shape
    assert norm_ffn1.shape[0] == 1 and ds == 1
    aw = g_attn_out.shape[1]
    mw = g_mlstm_out.shape[1]
    nh_attn = aw // ATTN_HEAD_DIM
    w_buf = cache_k.shape[2]
    assert w_buf == WINDOW_MAX and w_buf % (max(DILATIONS) * STEPS) == 0

    ffn1 = _ffn_weights(norm_ffn1[0], w_ffn1_gate[0], w_ffn1_up[0], w_ffn1_down[0])
    ffn2 = _ffn_weights(norm_ffn2[0], w_ffn2_gate[0], w_ffn2_up[0], w_ffn2_down[0])
    w_in_bf = w_in[0].astype(BF16)
    ws = [w_in_bf[:, i * aw:(i + 1) * aw] for i in range(7)]
    ngate = 2 * MLSTM_HEADS
    wgt = jnp.pad(w_in_bf[:, 7 * aw:], ((0, 0), (0, LANES - ngate)))
    bg = jnp.pad(b_gate[0], (0, LANES - ngate)).reshape(1, LANES)
    g_mix = norm_mix.reshape(1, d)
    g_attn = g_attn_out.reshape(1, aw)
    g_ml = g_mlstm_out.reshape(1, mw)
    wo = w_out[0].astype(BF16)
    out_consts = (wo[:aw], wo[aw:], *ffn2, norm_final.reshape(1, d))

    def inproj_part(x1, pos, tm, **kw):
        return _inproj_part(x1, g_mix, ws, wgt, bg, _rope_tables(pos), tm, **kw)

    xs = x_sample.reshape(db, d)
    (x1s,), = _call("ffn", [_ffn_part(xs, *ffn1, tm=db)])
    (qs, ks, vs, mqs, mks, mvs, mos, gts), = _call("inproj", [inproj_part(x1s, jnp.full((db,), PAST_LEN), db)])
    tok_t = lambda t: jnp.transpose(t.reshape(db, nh_attn, ATTN_HEAD_DIM), (0, 2, 1))
    rows_minor = lambda c: jnp.transpose(c[0], (0, 2, 3, 1))
    cache = (tok_t(qs), tok_t(ks), tok_t(vs), g_attn.reshape(nh_attn, ATTN_HEAD_DIM).T,
             rows_minor(cache_k), rows_minor(cache_v))

    mp = batch * s_len
    xp = x_prompt.reshape(mp, d)
    tm = _row_tile(s_len, ROW_TILE)
    chunk = _row_tile(s_len, MLSTM_CHUNK)
    keep = min(WINDOW_MAX, s_len)
    pos_p = jnp.arange(s_len)
    tiles = mp // tm
    half = batch // 2
    hosted = tiles == db and nh_attn % CACHE_PARTS == 0 and CACHE_PARTS == 2
    split = (hosted and batch % (2 * MLSTM_SEQS) == 0
             and (half // MLSTM_SEQS) * (s_len // chunk) == tiles // 2)

    if split:
        rows_h = half * s_len
        db_h = db // 2
        (x1,), (a00, k_t, v_t) = _call("ffn", [_ffn_part(xp, *ffn1, tm=tm, nrows=rows_h),
                                               _cache_part(cache, 0, CACHE_PARTS, 0, db_h)])
        proj_a, (a10, k_t, v_t) = _call("inproj", [
            inproj_part(x1, pos_p, tm, nrows=rows_h, seq_len=s_len, keep=keep),
            _cache_part(cache, 1, CACHE_PARTS, 0, db_h, prev=(k_t, v_t))])
        (x1,), ml_a = _call("ffn", [_ffn_part(xp, *ffn1, tm=tm, row0=rows_h, nrows=rows_h, y_prev=x1),
                                    _mlstm_part(*proj_a[3:8], g_ml, half, chunk)])
        proj_b, (a01, k_t, v_t) = _call("inproj", [
            inproj_part(x1, pos_p, tm, row0=rows_h, nrows=rows_h, seq_len=s_len, keep=keep,
                        window=proj_a[8:10]),
            _cache_part(cache, 0, CACHE_PARTS, db_h, db_h, prev=(k_t, v_t))])
        kw_t, vw_t = proj_b[8:10]
        att_a = _attention(*proj_a[:3], g_attn, half)
        att_b = _attention(*proj_b[:3], g_attn, half)
        mm_a, c_a, n_a, m_a = _mlstm_results(ml_a, half)
        (y_full,), ml_b = _call("out_ffn", [_out_ffn_part(x1, att_a, mm_a, out_consts, tm),
                                            _mlstm_part(*proj_b[3:8], g_ml, half, chunk)])
        mm_b, c_b, n_b, m_b = _mlstm_results(ml_b, half)
        (y_prompt,), (a11, k_t, v_t) = _call("out_ffn", [
            _out_ffn_part(x1, att_b, mm_b, out_consts, tm, row0=rows_h, y_prev=y_full),
            _cache_part(cache, 1, CACHE_PARTS, db_h, db_h, prev=(k_t, v_t))])
        c_p, n_p, m_p = (jnp.concatenate(pair, axis=0) for pair in ((c_a, c_b), (n_a, n_b), (m_a, m_b)))
        a_t = jnp.concatenate([jnp.concatenate([a00, a01], axis=0), jnp.concatenate([a10, a11], axis=0)],
                              axis=-1)
    else:
        if hosted:
            (x1,), (a_lo, k_t, v_t) = _call("ffn", [_ffn_part(xp, *ffn1, tm=tm),
                                                   _cache_part(cache, 0, CACHE_PARTS)])
        else:
            (x1,), = _call("ffn", [_ffn_part(xp, *ffn1, tm=tm)])
        proj, = _call("inproj", [inproj_part(x1, pos_p, tm, seq_len=s_len, keep=keep)])
        kw_t, vw_t = proj[8:10]
        att = _attention(*proj[:3], g_attn, batch)
        ml, = _call("mlstm", [_mlstm_part(*proj[3:8], g_ml, batch, chunk)])
        mm, c_p, n_p, m_p = _mlstm_results(ml, batch)
        if hosted:
            (y_prompt,), (a_hi, k_t, v_t) = _call("out_ffn", [
                _out_ffn_part(x1, att, mm, out_consts, tm),
                _cache_part(cache, 1, CACHE_PARTS, prev=(k_t, v_t))])
            a_t = jnp.concatenate([a_lo, a_hi], axis=-1)
        else:
            (y_prompt,), = _call("out_ffn", [_out_ffn_part(x1, att, mm, out_consts, tm)])
            (a_t, k_t, v_t), = _call("sample_cache", [_cache_part(cache)])
    y_prompt = y_prompt.reshape(batch, s_len, d)
    window = lambda t: jnp.transpose(t.reshape(batch, nh_attn, ATTN_HEAD_DIM, keep), (0, 3, 1, 2))
    k_prompt, v_prompt = window(kw_t), window(vw_t)

    a_s = jnp.transpose(a_t, (0, 2, 1)).reshape(db, aw).astype(BF16)
    k_s, v_s = jnp.transpose(k_t, (0, 3, 1, 2)), jnp.transpose(v_t, (0, 3, 1, 2))
    mm_s, c_s, n_s, m_s = _mlstm_step(mqs, mks, mvs, mos, gts, g_ml, state_C, state_n.reshape(db, mw),
                                      state_m[0], bb=_row_tile(db, MLSTM_STEP_ROWS))
    n_s = n_s.reshape(state_n.shape)
    (y_sample,), = _call("out_ffn", [_out_ffn_part(x1s, a_s, mm_s, out_consts, db)])

    return (y_prompt, y_sample.reshape(db, 1, d),
            k_prompt[None], v_prompt[None], c_p[None], n_p[None], m_p[None],
            k_s[None], v_s[None], c_s, n_s, m_s[None, :, :MLSTM_HEADS])
```

```python
import functools
import inspect
import math
from typing import Callable, NamedTuple

import jax
import jax.numpy as jnp
from jax import lax
from jax.experimental import pallas as pl
from jax.experimental.pallas import tpu as pltpu

F32 = jnp.float32
BF16 = jnp.bfloat16

EPS = 1e-6
LOG2_E = math.log2(math.e)
ROPE_THETA = 500000.0
ATTN_HEAD_DIM = 64
ROT_DIM = ATTN_HEAD_DIM // 4
MLSTM_HEADS = 4
MLSTM_HEAD_DIM = 128
DILATIONS = (16, 4, 1)
STEPS = 128
WINDOW_MAX = 2048
PAST_LEN = 8192
FORGET_COL = MLSTM_HEADS

LANES = 128
SUBLANES = 8
MLSTM_STEP_ROWS = 16
FF_CHUNK = 256
ROW_TILE = 512
MLSTM_CHUNK = 256
MLSTM_SEQS = 2
ATTN_GROUP = 4
ATTN_CLASSES = 2
ATTN_UNROLL = 8
CACHE_PARTS = 2
VMEM_LIMIT = 58 * 1024 * 1024


class _Part(NamedTuple):
    body: Callable
    operands: tuple
    in_specs: tuple
    out_specs: tuple
    out_shape: tuple
    scratch: tuple = ()
    aliases: tuple = ()
    before: Callable = None
    after: Callable = None


def _run_parts(name, steps, parts):
    counts = [(len(p.operands), len(p.out_shape), len(p.scratch)) for p in parts]

    def kernel(*refs):
        it = iter(refs)
        ins = [[next(it) for _ in range(c[0])] for c in counts]
        outs = [[next(it) for _ in range(c[1])] for c in counts]
        scrs = [[next(it) for _ in range(c[2])] for c in counts]
        for p, i, o, s in zip(parts, ins, outs, scrs):
            if p.before is not None:
                p.before(i, o, s)
        running = [g for g in (p.body(i, o, s) for p, i, o, s in zip(parts, ins, outs, scrs))
                   if inspect.isgenerator(g)]
        while running:
            for g in list(running):
                try:
                    next(g)
                except StopIteration:
                    running.remove(g)
        for p, i, o, s in zip(parts, ins, outs, scrs):
            if p.after is not None:
                p.after(i, o, s)

    in_off = [sum(c[0] for c in counts[:k]) for k in range(len(parts))]
    out_off = [sum(c[1] for c in counts[:k]) for k in range(len(parts))]
    aliases = {in_off[k] + i: out_off[k] + o for k, p in enumerate(parts) for i, o in p.aliases}
    flat = lambda field: [x for p in parts for x in getattr(p, field)]
    res = pl.pallas_call(
        kernel,
        grid=(steps,),
        in_specs=flat("in_specs"),
        out_specs=flat("out_specs"),
        out_shape=flat("out_shape"),
        scratch_shapes=flat("scratch"),
        input_output_aliases=aliases,
        compiler_params=pltpu.CompilerParams(dimension_semantics=("arbitrary",),
                                             vmem_limit_bytes=VMEM_LIMIT),
        name=name,
    )(*flat("operands"))
    return [res[out_off[k]:out_off[k] + counts[k][1]] for k in range(len(parts))]


def _const_spec(shape):
    nd = len(shape)
    return pl.BlockSpec(shape, lambda *_: (0,) * nd, pipeline_mode=pl.Buffered(1))


def _rows(tm, width, blk0=0):
    return pl.BlockSpec((tm, width), lambda i: (i + blk0, 0))


def _rms(x, g):
    return x * lax.rsqrt(jnp.mean(x * x, axis=-1, keepdims=True) + EPS) * g


def _swiglu_ffn(x, g_ref, wg_ref, wu_ref, wd_ref, acc_ref):
    h = _rms(x, g_ref[...]).astype(BF16)
    for c in range(wg_ref.shape[1] // FF_CHUNK):
        cols = slice(c * FF_CHUNK, (c + 1) * FF_CHUNK)
        gt = jnp.dot(h, wg_ref[:, cols], preferred_element_type=F32)
        ut = jnp.dot(h, wu_ref[:, cols], preferred_element_type=F32)
        a = (gt * jax.nn.sigmoid(gt) * ut).astype(BF16)
        part = jnp.dot(a, wd_ref[cols, :], preferred_element_type=F32)
        if c == 0:
            acc_ref[...] = part
        else:
            acc_ref[...] += part
        yield
    return x + 0.5 * acc_ref[...]


def _ffn_body(ins, outs, scratch):
    x_ref, g_ref, wg_ref, wu_ref, wd_ref = ins[:5]
    outs[0][...] = yield from _swiglu_ffn(x_ref[...], g_ref, wg_ref, wu_ref, wd_ref, scratch[0])


def _ffn_part(x, g, wg, wu, wd, tm, row0=0, nrows=None, y_prev=None):
    m, d = x.shape
    nrows = m if nrows is None else nrows
    blk0 = row0 // tm
    operands = [x, g, wg, wu, wd]
    in_specs = [_rows(tm, d, blk0), _const_spec(g.shape), _const_spec(wg.shape), _const_spec(wu.shape),
                _const_spec(wd.shape)]
    aliases = ()
    if y_prev is not None:
        aliases = ((len(operands), 0),)
        operands.append(y_prev)
        in_specs.append(pl.BlockSpec(memory_space=pl.ANY))
    return nrows // tm, _Part(
        _ffn_body, tuple(operands), tuple(in_specs), (_rows(tm, d, blk0),),
        (jax.ShapeDtypeStruct((m, d), F32),), (pltpu.VMEM((tm, d), F32),), aliases)


def _rope(y, cos, s_up, s_dn):
    parts = []
    for i in range(y.shape[1] // LANES):
        ys = y[:, LANES * i:LANES * (i + 1)]
        parts.append(ys * cos + pltpu.roll(ys, LANES - ROT_DIM // 2, 1) * s_up
                     + pltpu.roll(ys, ROT_DIM // 2, 1) * s_dn)
    return jnp.concatenate(parts, axis=1)


def _inproj_body(ins, outs, scratch):
    (x_ref, g_ref, wq_ref, wk_ref, wv_ref, wmq_ref, wmk_ref, wmv_ref, wmo_ref, wgt_ref, bg_ref,
     cos_ref, sup_ref, sdn_ref) = ins[:14]
    q_ref, k_ref, v_ref, mq_ref, mk_ref, mv_ref, mo_ref, gt_ref = outs[:8]
    h = _rms(x_ref[...], g_ref[...]).astype(BF16)
    cos, s_up, s_dn = cos_ref[...], sup_ref[...], sdn_ref[...]

    def proj(w_ref):
        return jnp.dot(h, w_ref[...], preferred_element_type=F32)

    q_ref[...] = _rope(proj(wq_ref), cos, s_up, s_dn) * (ATTN_HEAD_DIM ** -0.5 * LOG2_E)
    yield
    k = _rope(proj(wk_ref), cos, s_up, s_dn)
    k_ref[...] = k
    yield
    v = proj(wv_ref)
    v_ref[...] = v
    if len(outs) > 8:
        outs[8][...] = k.T
        outs[9][...] = v.T
    yield
    mq_ref[...] = proj(wmq_ref).astype(BF16)
    yield
    mk_ref[...] = (proj(wmk_ref) * (MLSTM_HEAD_DIM ** -0.5)).astype(BF16)
    yield
    mv_ref[...] = proj(wmv_ref).astype(BF16)
    yield
    mo_ref[...] = proj(wmo_ref)
    gt_ref[...] = proj(wgt_ref) + bg_ref[...]


def _inproj_part(x, g, ws, wgt, bg, tabs, tm, row0=0, nrows=None, seq_len=None, keep=0, window=None):
    d = x.shape[1]
    nrows = x.shape[0] if nrows is None else nrows
    w = ws[0].shape[1]
    blk0 = row0 // tm
    nt = tabs[0].shape[0] // tm
    tab = pl.BlockSpec((tm, LANES), lambda i: ((i + blk0) % nt, 0))
    out_dt = (F32, F32, F32, BF16, BF16, BF16, F32)
    operands = [x, g, *ws, wgt, bg, *tabs]
    in_specs = ([_rows(tm, d, blk0), _const_spec(g.shape)] + [_const_spec(wi.shape) for wi in ws]
                + [_const_spec(wgt.shape), _const_spec(bg.shape), tab, tab, tab])
    out_specs = [_rows(tm, w)] * 7 + [_rows(tm, LANES)]
    out_shape = ([jax.ShapeDtypeStruct((nrows, w), dt) for dt in out_dt]
                 + [jax.ShapeDtypeStruct((nrows, LANES), F32)])
    aliases = ()
    if keep:
        assert seq_len % tm == 0 and keep % tm == 0 and row0 % seq_len == 0
        tiles_per_seq, first_kept, seq0 = seq_len // tm, (seq_len - keep) // tm, row0 // seq_len
        kv_t = pl.BlockSpec((None, w, tm), lambda i: (i // tiles_per_seq + seq0, 0,
                                                      jnp.maximum(i % tiles_per_seq - first_kept, 0)))
        out_specs += [kv_t, kv_t]
        out_shape += [jax.ShapeDtypeStruct((x.shape[0] // seq_len, w, keep), F32)] * 2
        if window is not None:
            aliases = ((len(operands), 8), (len(operands) + 1, 9))
            operands += list(window)
            in_specs += [pl.BlockSpec(memory_space=pl.ANY)] * 2
    return nrows // tm, _Part(_inproj_body, tuple(operands), tuple(in_specs), tuple(out_specs),
                              tuple(out_shape), (), aliases)


def _attn_kernel(q_ref, k_ref, v_ref, g_ref, o_ref, acc_ref, m_ref, l_ref, bias_ref):
    s_len = q_ref.shape[0]
    blk = STEPS
    lane = lax.broadcasted_iota(jnp.int32, (1, LANES), 1)
    head0 = lane < ATTN_HEAD_DIM
    qi = lax.broadcasted_iota(jnp.int32, (2 * blk, 2 * blk), 0) & (blk - 1)
    kj = lax.broadcasted_iota(jnp.int32, (2 * blk, 2 * blk), 1)
    band = (kj >= qi) & (kj <= qi + blk)
    bias_ref[0] = jnp.where(band & (kj >= blk), 0.0, -jnp.inf)
    bias_ref[1] = jnp.where(band, 0.0, -jnp.inf)

    ones_blk = jnp.ones((2 * blk, LANES), BF16)

    def pick(a):
        return jnp.where(head0, a[:blk], a[blk:])

    def branch(dil, first, last):
        nblk = s_len // (dil * blk)

        rows = lambda start: pl.ds(start, blk, stride=dil) if dil > 1 else pl.ds(start, blk)

        def run_units(units):
            old = [None if first else (m_ref[rows(qs_), :], l_ref[rows(qs_), :], acc_ref[rows(qs_), :])
                   for qs_, _, _ in units]
            loaded = {}

            def block(ref, start):
                key = (id(ref), id(start))
                if key not in loaded:
                    loaded[key] = ref[rows(start), :].astype(BF16)
                return loaded[key]

            cat = lambda parts: parts[0] if len(parts) == 1 else jnp.concatenate(parts, axis=0)
            scores, vals = [], []
            for qstart, kstarts, bias in units:
                q = q_ref[rows(qstart), :]
                kk = cat([block(k_ref, st) for st in kstarts])
                vv = cat([block(v_ref, st) for st in kstarts])
                vals.append(jnp.concatenate([vv, ones_blk[:vv.shape[0]]], axis=1))
                qs = jnp.concatenate([jnp.where(head0, q, 0.0), jnp.where(head0, 0.0, q)],
                                     axis=0).astype(BF16)
                s = lax.dot_general(qs, kk, (((1,), (1,)), ((), ())), preferred_element_type=F32)
                scores.append(s + bias)
            stats = []
            for s in scores:
                mb = jnp.max(s, axis=1, keepdims=True)
                p = jnp.exp2(s - mb)
                stats.append((mb, p.astype(BF16)))
            outs = [jnp.dot(p, vv, preferred_element_type=F32) for (_, p), vv in zip(stats, vals)]
            for (qstart, _, _), prev, (mb, _), res in zip(units, old, stats, outs):
                mb = pick(jnp.broadcast_to(mb, (2 * blk, LANES)))
                ob = pick(res[:, :LANES])
                lb = pick(res[:, LANES:])
                if first:
                    m_new, l_new, acc = mb, lb, ob
                else:
                    m_old, l_old, acc_old = prev
                    m_new = jnp.maximum(m_old, mb)
                    w_old = jnp.exp2(m_old - m_new)
                    w_blk = jnp.exp2(mb - m_new)
                    l_new = l_old * w_old + lb * w_blk
                    acc = acc_old * w_old + ob * w_blk
                if last:
                    y = acc / l_new
                    sq = y * y
                    ms0 = jnp.sum(jnp.where(head0, sq, 0.0), axis=1, keepdims=True)
                    ms1 = jnp.sum(jnp.where(head0, 0.0, sq), axis=1, keepdims=True)
                    ms = jnp.where(head0, ms0, ms1) * (1.0 / ATTN_HEAD_DIM)
                    o_ref[rows(qstart), :] = (y * lax.rsqrt(ms + EPS) * g_ref[...]).astype(o_ref.dtype)
                else:
                    m_ref[rows(qstart), :] = m_new
                    l_ref[rows(qstart), :] = l_new
                    acc_ref[rows(qstart), :] = acc

        def align(start):
            return pl.multiple_of(start, blk) if dil == 1 else start

        if nblk * ATTN_CLASSES <= ATTN_GROUP:
            def classes(ci, carry):
                units = []
                for j in range(ATTN_CLASSES):
                    starts = [align(ci * ATTN_CLASSES + j + dil * blk * n) for n in range(nblk)]
                    units.append((starts[0], starts[:1], bias_ref[1][:, blk:]))
                    units += [(starts[n], starts[n - 1:n + 1], bias_ref[1]) for n in range(1, nblk)]
                run_units(units)
                return carry

            assert dil % ATTN_CLASSES == 0
            lax.fori_loop(0, dil // ATTN_CLASSES, classes, 0, unroll=ATTN_UNROLL)
        else:
            def group(gi, carry):
                units = []
                for j in range(ATTN_GROUP):
                    u = gi * ATTN_GROUP + j
                    r = u // nblk
                    n = u % nblk
                    units.append((align(r + dil * blk * n),
                                  [align(r + dil * blk * jnp.maximum(n - 1, 0)), align(r + dil * blk * n)],
                                  bias_ref[jnp.minimum(n, 1)]))
                run_units(units)
                return carry

            assert (dil * nblk) % ATTN_GROUP == 0
            lax.fori_loop(0, dil * nblk // ATTN_GROUP, group, 0, unroll=ATTN_UNROLL)

    for idx, dil in enumerate(DILATIONS):
        branch(dil, idx == 0, idx == len(DILATIONS) - 1)


def _attention(q, k, v, g, batch):
    m, w = q.shape
    s_len = m // batch
    assert s_len % (max(DILATIONS) * STEPS) == 0
    nslab = w // LANES
    blk = pl.BlockSpec((s_len, LANES), lambda b, j: (b, j))
    return pl.pallas_call(
        _attn_kernel,
        grid=(batch, nslab),
        in_specs=[blk, blk, blk, pl.BlockSpec((1, LANES), lambda b, j: (0, j))],
        out_specs=blk,
        out_shape=jax.ShapeDtypeStruct((m, w), BF16),
        scratch_shapes=[pltpu.VMEM((s_len, LANES), F32)] * 3 + [pltpu.VMEM((2, 2 * STEPS, 2 * STEPS), F32)],
        compiler_params=pltpu.CompilerParams(dimension_semantics=("parallel", "parallel"),
                                             vmem_limit_bytes=VMEM_LIMIT),
        name="attention",
    )(q, k, v, g)


def _log_sigmoid(x):
    return jnp.minimum(x, 0.0) - jnp.log1p(jnp.exp(-jnp.abs(x)))


def _head_out(h, mo, g):
    hm = jax.nn.sigmoid(mo) * h
    return hm * lax.rsqrt(jnp.mean(hm * hm, axis=-1, keepdims=True) + EPS) * g


def _cummax_rows(x):
    row = lax.broadcasted_iota(jnp.int32, x.shape, 0)
    k = 1
    while k < x.shape[0]:
        x = jnp.maximum(x, jnp.where(row >= k, pltpu.roll(x, k, 0), -jnp.inf))
        k *= 2
    return x


def _mlstm_body(ins, outs, scratch):
    q_ref, k_ref, v_ref, mo_ref, gt_ref, g_ref = ins
    o_ref = outs[0]
    c_sc, m_sc = scratch
    nseq, length = q_ref.shape[:2]
    hd = MLSTM_HEAD_DIM

    ti = lax.broadcasted_iota(jnp.int32, (length, length), 0)
    si = lax.broadcasted_iota(jnp.int32, (length, length), 1)
    causal = si <= ti
    tril = causal.astype(BF16)
    one_col = (lax.broadcasted_iota(jnp.int32, (length, hd), 1) == 0).astype(BF16)

    gate_vals = []
    for s in range(nseq):
        gates = gt_ref[s]
        lf = _log_sigmoid(gates)
        hi = lf.astype(BF16)
        rem = lf - hi.astype(F32)
        mid = rem.astype(BF16)
        lo = (rem - mid.astype(F32)).astype(BF16)
        bcum = (jnp.dot(tril, hi, preferred_element_type=F32)
                + jnp.dot(tril, mid, preferred_element_type=F32)
                + jnp.dot(tril, lo, preferred_element_type=F32))
        b = pltpu.roll(bcum, LANES - FORGET_COL, 1)
        a = gates - b
        m_prev = m_sc[s][0:1, :]
        m_inter = b + m_prev
        m = jnp.maximum(m_inter, b + _cummax_rows(a))
        w_inter = jnp.exp(m_inter - m)
        u = b - m
        floor = jnp.exp(-m)
        a_t = a.T
        m_end = m[length - 1:length, :]
        b_end = b[length - 1:length, :]
        w_end = jnp.exp(a + (b_end - m_end))
        decay = jnp.exp(b_end + m_prev - m_end)
        m_sc[s] = jnp.broadcast_to(m_end, m_sc.shape[1:])
        gate_vals.append((w_inter, u, floor, a_t, w_end, decay))
        yield

    pairs = [(s, h) for s in range(nseq) for h in range(MLSTM_HEADS)]
    lanes = lambda h: slice(hd * h, hd * (h + 1))
    col = lambda h: slice(h, h + 1)
    nt_dims = (((1,), (1,)), ((), ()))
    tn_dims = (((0,), (0,)), ((), ()))
    qk_all = [lax.dot_general(q_ref[s, :, lanes(h)], k_ref[s, :, lanes(h)], nt_dims,
                              preferred_element_type=F32) for s, h in pairs]
    yield
    inter_all = [jnp.dot(q_ref[s, :, lanes(h)], c_sc[s, h].astype(BF16), preferred_element_type=F32)
                 for s, h in pairs]
    yield
    w_all = []
    for (s, h), qk in zip(pairs, qk_all):
        _, u, _, a_t, _, _ = gate_vals[s]
        w_all.append((jnp.exp(jnp.minimum(u[:, col(h)] + a_t[col(h), :], 0.0))
                      * jnp.where(causal, qk, 0.0)).astype(BF16))
    v_one = {(s, h): jnp.concatenate([v_ref[s, :, lanes(h)], one_col], axis=1) for s, h in pairs}
    yield
    intra_all = [jnp.dot(w, v_one[p], preferred_element_type=F32) for p, w in zip(pairs, w_all)]
    yield
    for (s, h), inter, intra in zip(pairs, inter_all, intra_all):
        w_inter, _, floor, _, _, _ = gate_vals[s]
        both = w_inter[:, col(h)] * inter + intra
        hh = both[:, :hd] / jnp.maximum(jnp.abs(both[:, hd:hd + 1]), floor[:, col(h)])
        o_ref[s, :, lanes(h)] = _head_out(hh, mo_ref[s, :, lanes(h)], g_ref[:, lanes(h)]).astype(o_ref.dtype)
    yield
    for s, h in pairs:
        _, _, _, _, w_end, decay = gate_vals[s]
        kw = (k_ref[s, :, lanes(h)].astype(F32) * w_end[:, col(h)]).astype(BF16)
        c_sc[s, h] = (decay[:, col(h)] * c_sc[s, h]
                      + lax.dot_general(kw, v_one[s, h], tn_dims, preferred_element_type=F32))


def _mlstm_reset(ins, outs, scratch, *, nck):
    @pl.when(pl.program_id(0) % nck == 0)
    def _():
        for ref in scratch:
            ref[...] = jnp.zeros_like(ref)


def _mlstm_finish(ins, outs, scratch, *, nck):
    _, c_out, n_out, m_out = outs
    c_sc, m_sc = scratch
    hd = MLSTM_HEAD_DIM

    @pl.when(pl.program_id(0) % nck == nck - 1)
    def _():
        m_out[...] = m_sc[...]
        for s in range(c_sc.shape[0]):
            for h in range(MLSTM_HEADS):
                c_fin = c_sc[s, h]
                c_out[s, h] = c_fin[:, :hd]
                n_out[s, h:h + 1, :] = c_fin[:, hd:].T[0:1, :]


def _mlstm_part(mq, mk, mv, mo, gates, g, batch, chunk_len):
    m, w = mq.shape
    s_len = m // batch
    nck = s_len // chunk_len
    nh, hd = MLSTM_HEADS, MLSTM_HEAD_DIM
    nseq = MLSTM_SEQS if batch % MLSTM_SEQS == 0 else 1
    groups = batch // nseq
    seqs = lambda t: t.reshape(groups, nseq, s_len, t.shape[-1])
    row = lambda width: pl.BlockSpec((None, nseq, chunk_len, width), lambda i: (i // nck, 0, i % nck, 0))
    state = lambda *dims: pl.BlockSpec((None, nseq) + dims, lambda i: (i // nck, 0) + (0,) * len(dims))
    return groups * nck, _Part(
        _mlstm_body,
        (seqs(mq), seqs(mk), seqs(mv), seqs(mo), seqs(gates), g),
        (row(w), row(w), row(w), row(w), row(LANES), pl.BlockSpec((1, w), lambda i: (0, 0))),
        (row(w), state(nh, hd, hd), state(nh, hd), state(SUBLANES, LANES)),
        (jax.ShapeDtypeStruct((groups, nseq, s_len, w), BF16),
         jax.ShapeDtypeStruct((groups, nseq, nh, hd, hd), F32),
         jax.ShapeDtypeStruct((groups, nseq, nh, hd), F32),
         jax.ShapeDtypeStruct((groups, nseq, SUBLANES, LANES), F32)),
        (pltpu.VMEM((nseq, nh, hd, 2 * hd), F32), pltpu.VMEM((nseq, SUBLANES, LANES), F32)),
        (), functools.partial(_mlstm_reset, nck=nck), functools.partial(_mlstm_finish, nck=nck))


def _mlstm_results(res, batch):
    h_out, c_fin, n_fin, m_fin = res
    nh, hd = MLSTM_HEADS, MLSTM_HEAD_DIM
    return (h_out.reshape(-1, h_out.shape[-1]), c_fin.reshape(batch, nh, hd, hd),
            n_fin.reshape(batch, nh, hd), m_fin.reshape(batch, SUBLANES, LANES)[:, 0, :nh])


def _out_ffn_body(ins, outs, scratch):
    x_ref, a_ref, mm_ref, woa_ref, wom_ref, g_ref, wg_ref, wu_ref, wd_ref, gf_ref = ins[:10]
    x = (x_ref[...] + jnp.dot(a_ref[...], woa_ref[...], preferred_element_type=F32)
         + jnp.dot(mm_ref[...], wom_ref[...], preferred_element_type=F32))
    yield
    x = yield from _swiglu_ffn(x, g_ref, wg_ref, wu_ref, wd_ref, scratch[0])
    outs[0][...] = _rms(x, gf_ref[...])


def _out_ffn_part(x, a, mm, consts, tm, row0=0, y_prev=None):
    m, d = x.shape
    blk0 = row0 // tm
    operands = [x, a, mm, *consts]
    in_specs = ([_rows(tm, d, blk0), _rows(tm, a.shape[1]), _rows(tm, mm.shape[1])]
                + [_const_spec(c.shape) for c in consts])
    aliases = ()
    if y_prev is not None:
        aliases = ((len(operands), 0),)
        operands.append(y_prev)
        in_specs.append(pl.BlockSpec(memory_space=pl.ANY))
    return a.shape[0] // tm, _Part(
        _out_ffn_body, tuple(operands), tuple(in_specs), (_rows(tm, d, blk0),),
        (jax.ShapeDtypeStruct((m, d), F32),), (pltpu.VMEM((tm, d), F32),), aliases)


def _sample_cache_body(ins, outs, scratch):
    q_ref, kn_ref, vn_ref, g_ref, ck_ref, cv_ref = ins[:6]
    o_ref, ok_ref, ov_ref = outs
    nh, hd, w = ck_ref.shape
    q, kn, vn = q_ref[...], kn_ref[...], vn_ref[...]
    head_row = lax.broadcasted_iota(jnp.int32, (nh, 1), 0)
    head_lane = lax.broadcasted_iota(jnp.int32, (1, nh), 1)
    dist = w - lax.broadcasted_iota(jnp.int32, (1, w), 1)

    s = jnp.zeros((nh, w), F32)
    s_self = jnp.zeros((nh, 1), F32)
    for h in range(nh):
        qc = q[:, h:h + 1]
        s = jnp.where(head_row == h, jnp.sum(ck_ref[h] * qc, axis=0, keepdims=True), s)
        s_self = jnp.where(head_row == h, jnp.sum(qc * kn[:, h:h + 1], axis=0, keepdims=True), s_self)

    ms, ps, pselfs, ls = [], [], [], []
    for dil in DILATIONS:
        in_branch = ((dist & (dil - 1)) == 0) & (dist <= STEPS * dil)
        sm = jnp.where(in_branch, s, -jnp.inf)
        mb = jnp.maximum(jnp.max(sm, axis=1, keepdims=True), s_self)
        p = jnp.exp2(sm - mb)
        p_self = jnp.exp2(s_self - mb)
        ms.append(mb)
        ps.append(p)
        pselfs.append(p_self)
        ls.append(jnp.sum(p, axis=1, keepdims=True) + p_self)
    m_max = functools.reduce(jnp.maximum, ms)
    ws = [jnp.exp2(mb - m_max) for mb in ms]
    p_all = sum(wb * p for wb, p in zip(ws, ps))
    p_new = sum(wb * p for wb, p in zip(ws, pselfs))
    den = sum(wb * lb for wb, lb in zip(ws, ls))

    y = jnp.zeros((hd, nh), F32)
    for h in range(nh):
        o_h = (jnp.sum(cv_ref[h] * p_all[h:h + 1, :], axis=1, keepdims=True)
               + vn[:, h:h + 1] * p_new[h:h + 1, :])
        y = jnp.where(head_lane == h, o_h / den[h:h + 1, :], y)
    o_ref[...] = y * lax.rsqrt(jnp.mean(y * y, axis=0, keepdims=True) + EPS) * g_ref[...]

    last_lane = lax.broadcasted_iota(jnp.int32, (1, LANES), 1) == LANES - 1
    for c_ref, new, out_ref in ((ck_ref, kn, ok_ref), (cv_ref, vn, ov_ref)):
        for h in range(nh):
            shifted = pltpu.roll(c_ref[h], w - 1, 1)
            out_ref[h, :, :w - LANES] = shifted[:, :w - LANES]
            out_ref[h, :, w - LANES:] = jnp.where(last_lane, new[:, h:h + 1], shifted[:, w - LANES:])


def _cache_part(cache, part=0, nparts=1, seq0=0, nseq=None, prev=None):
    q_t, kn_t, vn_t, g_t, ck, cv = cache
    b, nh, hd, w = ck.shape
    nseq = b if nseq is None else nseq
    nloc = nh // nparts
    heads = slice(part * nloc, (part + 1) * nloc)
    tok = pl.BlockSpec((None, hd, nloc), lambda i: (i + seq0, 0, 0))
    blk = pl.BlockSpec((None, nloc, hd, w), lambda i: (i + seq0, part, 0, 0))
    operands = [q_t[..., heads], kn_t[..., heads], vn_t[..., heads], g_t[:, heads], ck, cv]
    in_specs = [tok, tok, tok, pl.BlockSpec((hd, nloc), lambda i: (0, 0)), blk, blk]
    aliases = ()
    if prev is not None:
        aliases = ((len(operands), 1), (len(operands) + 1, 2))
        operands += list(prev)
        in_specs += [pl.BlockSpec(memory_space=pl.ANY)] * 2
    return nseq, _Part(
        _sample_cache_body, tuple(operands), tuple(in_specs),
        (pl.BlockSpec((None, hd, nloc), lambda i: (i, 0, 0)), blk, blk),
        (jax.ShapeDtypeStruct((nseq, hd, nloc), F32),) + (jax.ShapeDtypeStruct(ck.shape, F32),) * 2,
        (), aliases)


def _mlstm_step_kernel(q_ref, k_ref, v_ref, mo_ref, gt_ref, g_ref, c_ref, n_ref, m_ref,
                       o_ref, c_out, n_out, m_out):
    bb = q_ref.shape[0]
    hd = MLSTM_HEAD_DIM
    gates = gt_ref[...]
    lane = lax.broadcasted_iota(jnp.int32, (bb, LANES), 1)
    eye = (lax.broadcasted_iota(jnp.int32, (hd, hd), 0)
           == lax.broadcasted_iota(jnp.int32, (hd, hd), 1)).astype(BF16)
    m_all = jnp.zeros((bb, LANES), F32)
    for h in range(MLSTM_HEADS):
        sl = slice(hd * h, hd * (h + 1))
        q, k, v = q_ref[:, sl], k_ref[:, sl], v_ref[:, sl]
        qf, kf, vf = q.astype(F32), k.astype(F32), v.astype(F32)
        ig = gates[:, h:h + 1]
        lf = _log_sigmoid(gates[:, FORGET_COL + h:FORGET_COL + h + 1])
        m_prev = m_ref[:, h:h + 1]
        n_prev = n_ref[:, sl]
        m_inter = lf + m_prev
        m = jnp.maximum(m_inter, ig)
        w_in = jnp.exp(ig - m)
        w_inter = jnp.exp(m_inter - m)
        q_t = lax.dot_general(eye, q, (((1,), (1,)), ((), ())), preferred_element_type=F32)
        k_t = lax.dot_general(eye, k, (((1,), (1,)), ((), ())), preferred_element_type=F32)
        qc_rows = []
        for i in range(bb):
            c_prev = c_ref[0, i, h]
            c_bf = c_prev.astype(BF16).astype(F32)
            qc_rows.append(jnp.sum(q_t[:, i:i + 1] * c_bf, axis=0, keepdims=True))
            c_out[0, i, h] = (w_inter[i:i + 1, :] * c_prev
                              + (w_in[i:i + 1, :] * k_t[:, i:i + 1]) * vf[i:i + 1, :])
        qc = jnp.concatenate(qc_rows, axis=0)
        w_intra = w_in * jnp.sum(qf * kf, axis=1, keepdims=True)
        num = w_inter * qc + w_intra * vf
        den = w_inter * jnp.sum(qf * n_prev, axis=1, keepdims=True) + w_intra
        hh = num / jnp.maximum(jnp.abs(den), jnp.exp(-m))
        o_ref[:, sl] = _head_out(hh, mo_ref[:, sl], g_ref[:, sl]).astype(o_ref.dtype)
        n_out[:, sl] = w_inter * n_prev + w_in * kf
        m_all = jnp.where(lane == h, m, m_all)
    m_out[...] = m_all


def _mlstm_step(mq, mk, mv, mo, gates, g, state_c, state_n, state_m, bb):
    b, w = mq.shape
    nh, hd = MLSTM_HEADS, MLSTM_HEAD_DIM
    row = lambda width: pl.BlockSpec((bb, width), lambda i: (i, 0))
    c_spec = pl.BlockSpec((1, bb, nh, hd, hd), lambda i: (0, i, 0, 0, 0))
    return pl.pallas_call(
        _mlstm_step_kernel,
        grid=(b // bb,),
        in_specs=[row(w), row(w), row(w), row(w), row(LANES), pl.BlockSpec((1, w), lambda i: (0, 0)),
                  c_spec, row(w), row(nh)],
        out_specs=[row(w), c_spec, row(w), row(LANES)],
        out_shape=[jax.ShapeDtypeStruct((b, w), BF16),
                   jax.ShapeDtypeStruct(state_c.shape, F32),
                   jax.ShapeDtypeStruct((b, w), F32),
                   jax.ShapeDtypeStruct((b, LANES), F32)],
        compiler_params=pltpu.CompilerParams(dimension_semantics=("parallel",),
                                             vmem_limit_bytes=VMEM_LIMIT),
        name="mlstm_step",
    )(mq, mk, mv, mo, gates, g, state_c, state_n, state_m)


def _rope_tables(pos):
    half = ROT_DIM // 2
    inv = jnp.power(jnp.float32(ROPE_THETA), -jnp.arange(half, dtype=F32) * 2.0 / ROT_DIM)
    ang = pos.astype(F32)[:, None] * inv[None, :]
    cos, sin = jnp.cos(ang), jnp.sin(ang)
    rest = ATTN_HEAD_DIM - ROT_DIM
    one = jnp.ones((pos.shape[0], rest), F32)
    zero = jnp.zeros((pos.shape[0], rest), F32)
    zh = jnp.zeros_like(sin)
    head = lambda parts: jnp.tile(jnp.concatenate(parts, axis=1), (1, LANES // ATTN_HEAD_DIM))
    return head([cos, cos, one]), head([-sin, zh, zero]), head([zh, sin, zero])


def _ffn_weights(g, w_gate, w_up, w_down):
    d, f = w_gate.shape
    assert f % FF_CHUNK == 0
    return g.reshape(1, d), w_gate.astype(BF16), w_up.astype(BF16), w_down.astype(BF16)


def _row_tile(m, want):
    return want if m % want == 0 else m


def _call(name, steps_and_parts):
    steps = {s for s, _ in steps_and_parts}
    assert len(steps) == 1, (name, steps)
    return _run_parts(name, steps.pop(), [p for _, p in steps_and_parts])


def kernel(x_prompt, x_sample, cache_k, cache_v, state_C, state_n, state_m, norm_ffn1, w_ffn1_gate,
           w_ffn1_up, w_ffn1_down, norm_mix, w_in, b_gate, g_attn_out, g_mlstm_out, w_out, norm_ffn2,
           w_ffn2_gate, w_ffn2_up, w_ffn2_down, norm_final):
    batch, s_len, d = x_prompt.shape
    db, ds, _ = x_sample.shape
    assert norm_ffn1.shape[0] == 1 and ds == 1
    aw = g_attn_out.shape[1]
    mw = g_mlstm_out.shape[1]
    nh_attn = aw // ATTN_HEAD_DIM
    w_buf = cache_k.shape[2]
    assert w_buf == WINDOW_MAX and w_buf % (max(DILATIONS) * STEPS) == 0

    ffn1 = _ffn_weights(norm_ffn1[0], w_ffn1_gate[0], w_ffn1_up[0], w_ffn1_down[0])
    ffn2 = _ffn_weights(norm_ffn2[0], w_ffn2_gate[0], w_ffn2_up[0], w_ffn2_down[0])
    w_in_bf = w_in[0].astype(BF16)
    ws = [w_in_bf[:, i * aw:(i + 1) * aw] for i in range(7)]
    ngate = 2 * MLSTM_HEADS
    wgt = jnp.pad(w_in_bf[:, 7 * aw:], ((0, 0), (0, LANES - ngate)))
    bg = jnp.pad(b_gate[0], (0, LANES - ngate)).reshape(1, LANES)
    g_mix = norm_mix.reshape(1, d)
    g_attn = g_attn_out.reshape(1, aw)
    g_ml = g_mlstm_out.reshape(1, mw)
    wo = w_out[0].astype(BF16)
    out_consts = (wo[:aw], wo[aw:], *ffn2, norm_final.reshape(1, d))

    def inproj_part(x1, pos, tm, **kw):
        return _inproj_part(x1, g_mix, ws, wgt, bg, _rope_tables(pos), tm, **kw)

    xs = x_sample.reshape(db, d)
    (x1s,), = _call("ffn", [_ffn_part(xs, *ffn1, tm=db)])
    (qs, ks, vs, mqs, mks, mvs, mos, gts), = _call("inproj", [inproj_part(x1s, jnp.full((db,), PAST_LEN), db)])
    tok_t = lambda t: jnp.transpose(t.reshape(db, nh_attn, ATTN_HEAD_DIM), (0, 2, 1))
    rows_minor = lambda c: jnp.transpose(c[0], (0, 2, 3, 1))
    cache = (tok_t(qs), tok_t(ks), tok_t(vs), g_attn.reshape(nh_attn, ATTN_HEAD_DIM).T,
             rows_minor(cache_k), rows_minor(cache_v))

    mp = batch * s_len
    xp = x_prompt.reshape(mp, d)
    tm = _row_tile(s_len, ROW_TILE)
    chunk = _row_tile(s_len, MLSTM_CHUNK)
    keep = min(WINDOW_MAX, s_len)
    pos_p = jnp.arange(s_len)
    tiles = mp // tm
    half = batch // 2
    hosted = tiles == db and nh_attn % CACHE_PARTS == 0 and CACHE_PARTS == 2
    split = (hosted and nh_attn % 4 == 0 and batch % (2 * MLSTM_SEQS) == 0
             and (half // MLSTM_SEQS) * (s_len // chunk) == tiles // 2)

    if split:
        rows_h = half * s_len
        db_h = db // 2
        (x1,), (a0, k_t, v_t) = _call("ffn", [_ffn_part(xp, *ffn1, tm=tm, nrows=rows_h),
                                              _cache_part(cache, 0, 2, 0, db_h)])
        proj_a, (a1, k_t, v_t) = _call("inproj", [
            inproj_part(x1, pos_p, tm, nrows=rows_h, seq_len=s_len, keep=keep),
            _cache_part(cache, 2, 4, 0, db_h, prev=(k_t, v_t))])
        (x1,), ml_a, (a2, k_t, v_t) = _call("ffn", [
            _ffn_part(xp, *ffn1, tm=tm, row0=rows_h, nrows=rows_h, y_prev=x1),
            _mlstm_part(*proj_a[3:8], g_ml, half, chunk),
            _cache_part(cache, 3, 4, 0, db_h, prev=(k_t, v_t))])
        proj_b, (b0, k_t, v_t) = _call("inproj", [
            inproj_part(x1, pos_p, tm, row0=rows_h, nrows=rows_h, seq_len=s_len, keep=keep,
                        window=proj_a[8:10]),
            _cache_part(cache, 0, 4, db_h, db_h, prev=(k_t, v_t))])
        kw_t, vw_t = proj_b[8:10]
        att_a = _attention(*proj_a[:3], g_attn, half)
        att_b = _attention(*proj_b[:3], g_attn, half)
        mm_a, c_a, n_a, m_a = _mlstm_results(ml_a, half)
        (y_full,), ml_b, (b1, k_t, v_t) = _call("out_ffn", [
            _out_ffn_part(x1, att_a, mm_a, out_consts, tm),
            _mlstm_part(*proj_b[3:8], g_ml, half, chunk),
            _cache_part(cache, 1, 4, db_h, db_h, prev=(k_t, v_t))])
        mm_b, c_b, n_b, m_b = _mlstm_results(ml_b, half)
        (y_prompt,), (b2, k_t, v_t) = _call("out_ffn", [
            _out_ffn_part(x1, att_b, mm_b, out_consts, tm, row0=rows_h, y_prev=y_full),
            _cache_part(cache, 1, 2, db_h, db_h, prev=(k_t, v_t))])
        c_p, n_p, m_p = (jnp.concatenate(pair, axis=0) for pair in ((c_a, c_b), (n_a, n_b), (m_a, m_b)))
        a_t = jnp.concatenate([jnp.concatenate([a0, a1, a2], axis=-1),
                               jnp.concatenate([b0, b1, b2], axis=-1)], axis=0)
    else:
        if hosted:
            (x1,), (a_lo, k_t, v_t) = _call("ffn", [_ffn_part(xp, *ffn1, tm=tm),
                                                   _cache_part(cache, 0, CACHE_PARTS)])
        else:
            (x1,), = _call("ffn", [_ffn_part(xp, *ffn1, tm=tm)])
        proj, = _call("inproj", [inproj_part(x1, pos_p, tm, seq_len=s_len, keep=keep)])
        kw_t, vw_t = proj[8:10]
        att = _attention(*proj[:3], g_attn, batch)
        ml, = _call("mlstm", [_mlstm_part(*proj[3:8], g_ml, batch, chunk)])
        mm, c_p, n_p, m_p = _mlstm_results(ml, batch)
        if hosted:
            (y_prompt,), (a_hi, k_t, v_t) = _call("out_ffn", [
                _out_ffn_part(x1, att, mm, out_consts, tm),
                _cache_part(cache, 1, CACHE_PARTS, prev=(k_t, v_t))])
            a_t = jnp.concatenate([a_lo, a_hi], axis=-1)
        else:
            (y_prompt,), = _call("out_ffn", [_out_ffn_part(x1, att, mm, out_consts, tm)])
            (a_t, k_t, v_t), = _call("sample_cache", [_cache_part(cache)])
    y_prompt = y_prompt.reshape(batch, s_len, d)
    window = lambda t: jnp.transpose(t.reshape(batch, nh_attn, ATTN_HEAD_DIM, keep), (0, 3, 1, 2))
    k_prompt, v_prompt = window(kw_t), window(vw_t)

    a_s = jnp.transpose(a_t, (0, 2, 1)).reshape(db, aw).astype(BF16)
    k_s, v_s = jnp.transpose(k_t, (0, 3, 1, 2)), jnp.transpose(v_t, (0, 3, 1, 2))
    mm_s, c_s, n_s, m_s = _mlstm_step(mqs, mks, mvs, mos, gts, g_ml, state_C, state_n.reshape(db, mw),
                                      state_m[0], bb=_row_tile(db, MLSTM_STEP_ROWS))
    n_s = n_s.reshape(state_n.shape)
    (y_sample,), = _call("out_ffn", [_out_ffn_part(x1s, a_s, mm_s, out_consts, db)])

    return (y_prompt, y_sample.reshape(db, 1, d),
            k_prompt[None], v_prompt[None], c_p[None], n_p[None], m_p[None],
            k_s[None], v_s[None], c_s, n_s, m_s[None, :, :MLSTM_HEADS])
```
